```python
import jax, jax.numpy as jnp
from jax import lax
import numpy as np

D_MODEL = 1024
BATCH = 4
SEQ = 8192
DEPTH = 1
DEC_BATCH = 32
DEC_SEQ = 2048
PAST_LEN = 128

GRID_W = 64
WIN_H = 8
WIN_W = 16
NA_HEADS = 8
NA_HEAD_DIM = 64
D_NA = NA_HEADS * NA_HEAD_DIM
RWKV_HEADS = 8
RWKV_HEAD_DIM = 64
D_RWKV = RWKV_HEADS * RWKV_HEAD_DIM
DECAY_LORA = 64
ICLR_LORA = 64
GATE_LORA = 128
N_DIR = 2
D_FF = 4 * D_MODEL
N_BRANCH = 2
D_SHIFT = 3 * D_RWKV + N_DIR * DECAY_LORA + N_DIR * ICLR_LORA + GATE_LORA
D_IN = 3 * D_NA + D_SHIFT + N_BRANCH * D_MODEL
RMS_EPS = 1e-6
GN_EPS = 64e-5

kernel_name = 'hybrid_natten_rwkv7_encoder'


def rmsnorm(x, g):
    xf = x.astype(jnp.float32)
    y = xf * lax.rsqrt(jnp.mean(xf * xf, axis=-1, keepdims=True) + RMS_EPS)
    return (y * g.astype(jnp.float32)).astype(x.dtype)


def neighbourhood_attention(q, k, v, rpb):
    b, L, _ = q.shape
    rows = L // GRID_W
    kh = min(WIN_H, rows)
    shp = (b, rows, GRID_W, NA_HEADS, NA_HEAD_DIM)
    qg = q.reshape(shp) * (NA_HEAD_DIM ** -0.5)
    kg = k.reshape(shp)
    vg = v.reshape(shp)
    r = jnp.arange(rows)
    row_start = jnp.clip(r - kh // 2, 0, rows - kh)
    key_rows = row_start[:, None] + jnp.arange(kh)[None, :]
    k_blk = kg[:, key_rows]
    v_blk = vg[:, key_rows]
    s = jnp.einsum('brqhd,brkwhd->bhrqkw', qg, k_blk).astype(jnp.float32)
    c = jnp.arange(GRID_W)
    col_start = jnp.clip(c - WIN_W // 2, 0, GRID_W - WIN_W)
    col_ok = (c[None, :] >= col_start[:, None]) & (c[None, :] < col_start[:, None] + WIN_W)
    row_idx = key_rows - r[:, None] + (WIN_H - 1)
    col_idx = jnp.clip(c[None, :] - c[:, None] + (WIN_W - 1), 0, 2 * WIN_W - 2)
    bias = rpb[:, row_idx[:, None, :, None], col_idx[None, :, None, :]]
    s = s + bias[None].astype(jnp.float32)
    s = jnp.where(col_ok[:, None, :], s, -jnp.inf)
    p = jax.nn.softmax(s.reshape(b, NA_HEADS, rows, GRID_W, kh * GRID_W), axis=-1)
    p = p.reshape(s.shape).astype(v.dtype)
    o = jnp.einsum('bhrqkw,brkwhd->brqhd', p, v_blk)
    return o.reshape(b, L, D_NA)


def centred_shift(p, mu_prev, mu_next):
    prev = jnp.pad(p[:, :-1], ((0, 0), (1, 0), (0, 0)))
    nxt = jnp.pad(p[:, 1:], ((0, 0), (0, 1), (0, 0)))
    return p + mu_prev * (prev - p) + mu_next * (nxt - p)


def rwkv7_scan(r, w, k, v, a, bb):
    b, L, nd, H, N = r.shape

    def step(S, inp):
        r_t, w_t, k_t, v_t, a_t, b_t = inp
        sa = jnp.einsum('zdhij,zdhj->zdhi', S, a_t)
        S = S * w_t[..., None, :] + sa[..., :, None] * b_t[..., None, :] + v_t[..., :, None] * k_t[..., None, :]
        y = jnp.einsum('zdhij,zdhj->zdhi', S, r_t)
        return S, y

    xs = (jnp.moveaxis(r, 1, 0), jnp.moveaxis(w, 1, 0), jnp.moveaxis(k, 1, 0),
          jnp.moveaxis(v, 1, 0), jnp.moveaxis(a, 1, 0), jnp.moveaxis(bb, 1, 0))
    S0 = jnp.zeros((b, nd, H, N, N), jnp.float32)
    _, y = lax.scan(step, S0, xs)
    return jnp.moveaxis(y, 0, 1)


def rwkv7_mixer(p, w0, w2, a0, a2, g2, k_k, k_a, r_k, ln_w, ln_b):
    b, L, _ = p.shape
    H, N = RWKV_HEADS, RWKV_HEAD_DIM
    f32 = jnp.float32
    r = p[..., 0:D_RWKV]
    k = p[..., D_RWKV:2 * D_RWKV]
    v = p[..., 2 * D_RWKV:3 * D_RWKV]
    off = 3 * D_RWKV
    wl = p[..., off:off + N_DIR * DECAY_LORA].reshape(b, L, N_DIR, DECAY_LORA)
    off = off + N_DIR * DECAY_LORA
    al = p[..., off:off + N_DIR * ICLR_LORA].reshape(b, L, N_DIR, ICLR_LORA)
    off = off + N_DIR * ICLR_LORA
    gl = p[..., off:off + GATE_LORA]
    w_log = -jax.nn.softplus(-(w0 + jnp.einsum('bldr,drc->bldc', jnp.tanh(wl), w2))) - 0.5
    decay = jnp.exp(-jnp.exp(w_log.astype(f32)))
    iclr = jax.nn.sigmoid((a0 + jnp.einsum('bldr,drc->bldc', al, a2)).astype(f32))
    g = jax.nn.sigmoid(gl) @ g2
    kk = (k * k_k).astype(f32).reshape(b, L, 1, H, N)
    kk = kk * lax.rsqrt(jnp.sum(kk * kk, axis=-1, keepdims=True) + 1e-12)
    k_dir = (k[:, :, None].astype(f32) * (1.0 + (iclr - 1.0) * k_a.astype(f32))).reshape(b, L, N_DIR, H, N)
    iclr_h = iclr.reshape(b, L, N_DIR, H, N)
    r_h = r.astype(f32).reshape(b, L, 1, H, N)
    v_h = v.astype(f32).reshape(b, L, 1, H, N)

    def both(t):
        t = jnp.broadcast_to(t.astype(f32), (b, L, N_DIR, H, N))
        return jnp.stack([t[:, :, 0], jnp.flip(t[:, :, 1], axis=1)], axis=2)

    y = rwkv7_scan(both(r_h), both(decay.reshape(b, L, N_DIR, H, N)), both(k_dir),
                   both(v_h), both(-kk), both(kk * iclr_h))
    y = y[:, :, 0] + jnp.flip(y[:, :, 1], axis=1)
    mean = jnp.mean(y, axis=-1, keepdims=True)
    var = jnp.mean(jnp.square(y - mean), axis=-1, keepdims=True)
    y = (y - mean) * lax.rsqrt(var + GN_EPS) * ln_w.astype(f32).reshape(H, N) + ln_b.astype(f32).reshape(H, N)
    bonus = jnp.sum(r_h * k_dir * r_k.astype(f32), axis=(2, 4))[..., None] * v_h[:, :, 0]
    out = (y + bonus).reshape(b, L, D_RWKV).astype(p.dtype)
    return out * g


def encoder_layer(x, w_in, b_gate, rpb, mu_prev, mu_next, w0, w2, a0, a2, g2, k_k, k_a, r_k,
                  ln_w, ln_b, w_proj_a, w_proj_b, w_out, w_up, w_down,
                  norm_pre_mix, norm_post_mix, norm_pre_ffn, norm_post_ffn):
    h = rmsnorm(x, norm_pre_mix)
    proj = jnp.einsum('bld,de->ble', h, w_in)
    q_na = proj[..., 0:D_NA]
    k_na = proj[..., D_NA:2 * D_NA]
    v_na = proj[..., 2 * D_NA:3 * D_NA]
    o_na = neighbourhood_attention(q_na, k_na, v_na, rpb)
    shifted = centred_shift(proj[..., 3 * D_NA:3 * D_NA + D_SHIFT], mu_prev, mu_next)
    o_rw = rwkv7_mixer(shifted, w0, w2, a0, a2, g2, k_k, k_a, r_k, ln_w, ln_b)
    gates = jax.nn.sigmoid(proj[..., 3 * D_NA + D_SHIFT:] + b_gate)
    merged = gates[..., :D_MODEL] * (o_na @ w_proj_a) + gates[..., D_MODEL:] * (o_rw @ w_proj_b)
    x = x + rmsnorm(merged @ w_out, norm_post_mix)
    h = rmsnorm(x, norm_pre_ffn)
    f = jnp.square(jax.nn.relu(h @ w_up)) @ w_down
    return x + rmsnorm(f, norm_post_ffn)


def trunk(x, layer_params):
    for i in range(DEPTH):
        x = encoder_layer(x, *[prm[i] for prm in layer_params])
    return x


def setup_inputs(seed: int = 0) -> dict:
    key = jax.random.key(seed)
    ks = jax.random.split(key, 32)
    f32 = jnp.float32

    def nrm(k, shape, scale):
        return jax.random.normal(k, shape, f32) * scale

    return {
        'x_prompt': nrm(ks[0], (BATCH, SEQ, D_MODEL), 1.0),
        'x_sample': nrm(ks[1], (DEC_BATCH, DEC_SEQ, D_MODEL), 1.0),
        'w_in': nrm(ks[2], (DEPTH, D_MODEL, D_IN), D_MODEL ** -0.5),
        'b_gate': nrm(ks[3], (DEPTH, N_BRANCH * D_MODEL), 0.1),
        'rpb': nrm(ks[4], (DEPTH, NA_HEADS, 2 * WIN_H - 1, 2 * WIN_W - 1), 0.1),
        'mu_prev': jax.random.uniform(ks[5], (DEPTH, D_SHIFT), f32, 0.0, 0.5),
        'mu_next': jax.random.uniform(ks[6], (DEPTH, D_SHIFT), f32, 0.0, 0.5),
        'w0': jax.random.uniform(ks[7], (DEPTH, N_DIR, D_RWKV), f32, -6.0, -0.5),
        'w2': nrm(ks[8], (DEPTH, N_DIR, DECAY_LORA, D_RWKV), 0.1),
        'a0': nrm(ks[9], (DEPTH, N_DIR, D_RWKV), 0.1),
        'a2': nrm(ks[10], (DEPTH, N_DIR, ICLR_LORA, D_RWKV), 0.1),
        'g2': nrm(ks[11], (DEPTH, GATE_LORA, D_RWKV), GATE_LORA ** -0.5),
        'k_k': 0.85 + nrm(ks[12], (DEPTH, D_RWKV), 0.02),
        'k_a': 1.0 + nrm(ks[13], (DEPTH, D_RWKV), 0.02),
        'r_k': nrm(ks[14], (DEPTH, RWKV_HEADS, RWKV_HEAD_DIM), 0.1),
        'ln_w': 1.0 + nrm(ks[15], (DEPTH, D_RWKV), 0.02),
        'ln_b': nrm(ks[16], (DEPTH, D_RWKV), 0.02),
        'w_proj_a': nrm(ks[17], (DEPTH, D_NA, D_MODEL), D_NA ** -0.5),
        'w_proj_b': nrm(ks[18], (DEPTH, D_RWKV, D_MODEL), D_RWKV ** -0.5),
        'w_out': nrm(ks[19], (DEPTH, D_MODEL, D_MODEL), D_MODEL ** -0.5),
        'w_up': nrm(ks[20], (DEPTH, D_MODEL, D_FF), D_MODEL ** -0.5),
        'w_down': nrm(ks[21], (DEPTH, D_FF, D_MODEL), D_FF ** -0.5),
        'norm_pre_mix': 1.0 + nrm(ks[22], (DEPTH, D_MODEL), 0.02),
        'norm_post_mix': 1.0 + nrm(ks[23], (DEPTH, D_MODEL), 0.02),
        'norm_pre_ffn': 1.0 + nrm(ks[24], (DEPTH, D_MODEL), 0.02),
        'norm_post_ffn': 1.0 + nrm(ks[25], (DEPTH, D_MODEL), 0.02),
    }


def reference(x_prompt, x_sample, w_in, b_gate, rpb, mu_prev, mu_next, w0, w2, a0, a2, g2,
              k_k, k_a, r_k, ln_w, ln_b, w_proj_a, w_proj_b, w_out, w_up, w_down,
              norm_pre_mix, norm_post_mix, norm_pre_ffn, norm_post_ffn):
    layer_params = (w_in, b_gate, rpb, mu_prev, mu_next, w0, w2, a0, a2, g2, k_k, k_a, r_k,
                    ln_w, ln_b, w_proj_a, w_proj_b, w_out, w_up, w_down,
                    norm_pre_mix, norm_post_mix, norm_pre_ffn, norm_post_ffn)
    y_prompt = trunk(x_prompt, layer_params)
    y_sample = trunk(x_sample, layer_params)
    return (y_prompt, y_sample)
```

```python
import functools
import math

import jax
import jax.numpy as jnp
from jax import lax
from jax.experimental import pallas as pl
from jax.experimental.pallas import tpu as pltpu

F32 = jnp.float32
BF16 = jnp.bfloat16

D_MODEL = 1024
GRID_W = 64
WIN_H = 8
WIN_W = 16
NA_HEADS = 8
HEAD_DIM = 64
D_NA = NA_HEADS * HEAD_DIM
RWKV_HEADS = 8
D_RWKV = RWKV_HEADS * HEAD_DIM
DECAY_LORA = 64
ICLR_LORA = 64
GATE_LORA = 128
N_DIR = 2
D_FF = 4 * D_MODEL
D_SHIFT = 3 * D_RWKV + N_DIR * DECAY_LORA + N_DIR * ICLR_LORA + GATE_LORA
RMS_EPS = 1e-6
GN_EPS = 64e-5
KK_EPS = 1e-12
NEG_BIG = -1e30

CHUNK = 64
PAIR = 2 * HEAD_DIM
N_PAIR = D_RWKV // PAIR
TOK_TILE = 512
ROWS_PER_TILE = TOK_TILE // GRID_W
CHUNKS_PER_TILE = TOK_TILE // CHUNK
FFN_TOK_TILE = 1024
FFN_COL_TILE = 512
V7X_VMEM_LIMIT = 56 * 1024 * 1024

_NT = (((1,), (1,)), ((), ()))
_TN = (((0,), (0,)), ((), ()))


def _dot(a, b):
    return jnp.dot(a, b, preferred_element_type=F32)


def _sigmoid(x):
    return 1.0 / (1.0 + jnp.exp(-x))


def _params(sem):
    return pltpu.CompilerParams(dimension_semantics=sem, vmem_limit_bytes=V7X_VMEM_LIMIT)


def _const_spec(shape):
    nd = len(shape)
    return pl.BlockSpec(shape, lambda *_: (0,) * nd)


def _inproj_body(x_ref, g_ref, wqkv_ref, wsh_ref, wgt_ref, bg_ref,
                 q_ref, k_ref, v_ref, ps_ref, gt_ref):
    x = x_ref[...]
    ms = jnp.mean(x * x, axis=-1, keepdims=True)
    h = (x * lax.rsqrt(ms + RMS_EPS) * g_ref[...]).astype(BF16)
    qkv = _dot(h, wqkv_ref[...])
    q_ref[...] = (qkv[:, 0:D_NA] * (HEAD_DIM ** -0.5)).astype(BF16)
    k_ref[...] = qkv[:, D_NA:2 * D_NA].astype(BF16)
    v_ref[...] = qkv[:, 2 * D_NA:3 * D_NA].astype(BF16)
    ps_ref[...] = _dot(h, wsh_ref[...])
    gt_ref[...] = _sigmoid(_dot(h, wgt_ref[...]) + bg_ref[...])


def _inproj(x, g, wqkv, wsh, wgt, bg):
    t = x.shape[0]
    tm = TOK_TILE
    row = lambda w: pl.BlockSpec((tm, w), lambda i: (i, 0))
    return pl.pallas_call(
        _inproj_body,
        grid=(t // tm,),
        in_specs=[row(D_MODEL), _const_spec(g.shape), _const_spec(wqkv.shape),
                  _const_spec(wsh.shape), _const_spec(wgt.shape), _const_spec(bg.shape)],
        out_specs=[row(D_NA), row(D_NA), row(D_NA), row(D_SHIFT), row(2 * D_MODEL)],
        out_shape=[jax.ShapeDtypeStruct((t, D_NA), BF16)] * 3
        + [jax.ShapeDtypeStruct((t, D_SHIFT), F32), jax.ShapeDtypeStruct((t, 2 * D_MODEL), F32)],
        compiler_params=_params(("parallel",)),
        name="inproj",
    )(x, g, wqkv, wsh, wgt, bg)


def _na_bias_table(rpb):
    c = jnp.arange(GRID_W)
    col_start = jnp.clip(c - WIN_W // 2, 0, GRID_W - WIN_W)
    col_ok = (c[None, :] >= col_start[:, None]) & (c[None, :] < col_start[:, None] + WIN_W)
    col_idx = jnp.clip(c[None, :] - c[:, None] + (WIN_W - 1), 0, 2 * WIN_W - 2)
    row_idx = (WIN_H - 1 - jnp.arange(WIN_H))[:, None] + jnp.arange(WIN_H)[None, :]
    tab = rpb[:, row_idx[:, None, :, None], col_idx[None, :, None, :]]
    tab = jnp.where(col_ok[None, None, :, None, :], tab.astype(F32), NEG_BIG)
    return tab.transpose(1, 0, 2, 3, 4).reshape(WIN_H, NA_HEADS, GRID_W, WIN_H * GRID_W)


def _natten_body(q_ref, kp_ref, kc_ref, kn_ref, vp_ref, vc_ref, vn_ref, bias_ref, o_ref,
                 kwin, vwin, *, rows):
    j = pl.program_id(1)
    kwin[0:TOK_TILE, :] = kp_ref[...]
    kwin[TOK_TILE:2 * TOK_TILE, :] = kc_ref[...]
    kwin[2 * TOK_TILE:3 * TOK_TILE, :] = kn_ref[...]
    vwin[0:TOK_TILE, :] = vp_ref[...]
    vwin[TOK_TILE:2 * TOK_TILE, :] = vc_ref[...]
    vwin[2 * TOK_TILE:3 * TOK_TILE, :] = vn_ref[...]
    first_half = lax.broadcasted_iota(jnp.int32, (GRID_W, PAIR), 1) < HEAD_DIM
    n_keys = WIN_H * GRID_W

    def row_body(jr, carry):
        r = j * ROWS_PER_TILE + jr
        rs = jnp.clip(r - WIN_H // 2, 0, rows - WIN_H)
        e = r - rs
        off = pl.multiple_of((rs - (j - 1) * ROWS_PER_TILE) * GRID_W, GRID_W)
        q0 = pl.multiple_of(jr * GRID_W, GRID_W)
        for p in range(N_PAIR):
            ls = slice(p * PAIR, (p + 1) * PAIR)
            kk = kwin[pl.ds(off, n_keys), ls]
            vv = vwin[pl.ds(off, n_keys), ls]
            qp = q_ref[pl.ds(q0, GRID_W), ls]
            outs = []
            for hh in range(2):
                msk = first_half if hh == 0 else jnp.logical_not(first_half)
                qm = jnp.where(msk, qp, jnp.zeros_like(qp))
                s = lax.dot_general(qm, kk, _NT, preferred_element_type=F32)
                s = s + bias_ref[e, 2 * p + hh]
                m = jnp.max(s, axis=-1, keepdims=True)
                pr = jnp.exp(s - m)
                l = jnp.sum(pr, axis=-1, keepdims=True)
                outs.append(_dot(pr.astype(BF16), vv) / l)
            o_ref[pl.ds(q0, GRID_W), ls] = jnp.where(first_half, outs[0], outs[1]).astype(BF16)
        return carry

    lax.fori_loop(0, ROWS_PER_TILE, row_body, 0)


def _natten(q, k, v, bias, b, seq):
    t = q.shape[0]
    rows = seq // GRID_W
    nblk = seq // TOK_TILE
    cur = lambda bi, j: (bi * nblk + j, 0)
    prev = lambda bi, j: (bi * nblk + jnp.maximum(j - 1, 0), 0)
    nxt = lambda bi, j: (bi * nblk + jnp.minimum(j + 1, nblk - 1), 0)
    blk = lambda im: pl.BlockSpec((TOK_TILE, D_NA), im)
    return pl.pallas_call(
        functools.partial(_natten_body, rows=rows),
        grid=(b, nblk),
        in_specs=[blk(cur), blk(prev), blk(cur), blk(nxt), blk(prev), blk(cur), blk(nxt),
                  _const_spec(bias.shape)],
        out_specs=blk(cur),
        out_shape=jax.ShapeDtypeStruct((t, D_NA), BF16),
        scratch_shapes=[pltpu.VMEM((3 * TOK_TILE, D_NA), BF16),
                        pltpu.VMEM((3 * TOK_TILE, D_NA), BF16)],
        compiler_params=_params(("parallel", "parallel")),
        name="natten",
    )(q, k, k, k, v, v, v, bias)


def _split3_dot(tri, x):
    x0 = x.astype(BF16)
    r1 = x - x0.astype(F32)
    x1 = r1.astype(BF16)
    x2 = (r1 - x1.astype(F32)).astype(BF16)
    return _dot(tri, x0) + _dot(tri, x1) + _dot(tri, x2)


def _prep_body(ps_ref, pp_ref, pn_ref, mup_ref, mun_ref, w0_ref, w2_ref, a0_ref, a2_ref, g2_ref,
               kkw_ref, ka_ref, rk_ref, seg_ref, tri_ref,
               rt_ref, kt_ref, bt_ref, at_ref, be_ref, ke_ref, v_ref, pend_ref, g_ref, bonus_ref,
               *, tiles_per_seq):
    i = pl.program_id(0)
    tb = TOK_TILE
    pos = lax.rem(i, tiles_per_seq)
    p = ps_ref[...]
    prev_row = jnp.where(pos == 0, 0.0, pp_ref[7:8, :])
    next_row = jnp.where(pos == tiles_per_seq - 1, 0.0, pn_ref[0:1, :])
    rowi = lax.broadcasted_iota(jnp.int32, (tb, 1), 0)
    prev = jnp.where(rowi == 0, prev_row, pltpu.roll(p, 1, 0))
    nxt = jnp.where(rowi == tb - 1, next_row, pltpu.roll(p, tb - 1, 0))
    sh = p + mup_ref[...] * (prev - p) + mun_ref[...] * (nxt - p)

    r = sh[:, 0:D_RWKV]
    k = sh[:, D_RWKV:2 * D_RWKV]
    v = sh[:, 2 * D_RWKV:3 * D_RWKV]
    off = 3 * D_RWKV
    wl = sh[:, off:off + 2 * DECAY_LORA]
    al = sh[:, off + 2 * DECAY_LORA:off + 2 * DECAY_LORA + 2 * ICLR_LORA]
    gl = sh[:, off + 2 * DECAY_LORA + 2 * ICLR_LORA:]

    zw = w0_ref[...] + _dot(jnp.tanh(wl).astype(BF16), w2_ref[...])
    za = a0_ref[...] + _dot(al.astype(BF16), a2_ref[...])
    g_ref[...] = _dot(_sigmoid(gl).astype(BF16), g2_ref[...])
    lw = (-math.exp(-0.5)) * _sigmoid(zw)
    iclr = _sigmoid(za)

    kkv = k * kkw_ref[...]
    kkn = kkv * lax.rsqrt(_dot((kkv * kkv).astype(BF16), seg_ref[...]) + KK_EPS)
    v_ref[...] = v.astype(BF16)

    bonus_acc = jnp.zeros_like(r)
    for d in range(N_DIR):
        ds_ = slice(d * D_RWKV, (d + 1) * D_RWKV)
        icl = iclr[:, ds_]
        kd = k * (1.0 + (icl - 1.0) * ka_ref[...])
        bd = kkn * icl
        bonus_acc = bonus_acc + r * kd * rk_ref[...]
        lwd = lw[:, ds_]
        tri = tri_ref[d]
        for c in range(CHUNKS_PER_TILE):
            cs = slice(c * CHUNK, (c + 1) * CHUNK)
            x = lwd[cs]
            lp = _split3_dot(tri, x)
            lend = lp[CHUNK - 1:CHUNK] if d == 0 else lp[0:1]
            pw = jnp.exp(lp)
            pinv = jnp.exp(-lp)
            pex = jnp.exp(lp - x)
            pe = jnp.exp(lend - lp)
            rt_ref[d, cs, :] = (r[cs] * pw).astype(BF16)
            kt_ref[d, cs, :] = (kd[cs] * pinv).astype(BF16)
            bt_ref[d, cs, :] = (bd[cs] * pinv).astype(BF16)
            at_ref[d, cs, :] = (-kkn[cs] * pex).astype(BF16)
            be_ref[d, cs, :] = (bd[cs] * pe).astype(BF16)
            ke_ref[d, cs, :] = (kd[cs] * pe).astype(BF16)
            pend_ref[d, c] = jnp.exp(lend)
    bonus_ref[...] = _dot(bonus_acc.astype(BF16), seg_ref[...]) * v


def _prep(ps, seq, mup, mun, w0, w2bd, a0, a2bd, g2, kkw, ka, rk, seg, tri):
    t = ps.shape[0]
    tb = TOK_TILE
    n8 = tb // 8
    last8 = t // 8 - 1
    row = lambda w: pl.BlockSpec((tb, w), lambda i: (i, 0))
    drow = pl.BlockSpec((N_DIR, tb, D_RWKV), lambda i: (0, i, 0))
    consts = [mup, mun, w0, w2bd, a0, a2bd, g2, kkw, ka, rk, seg, tri]
    bf = jax.ShapeDtypeStruct((N_DIR, t, D_RWKV), BF16)
    return pl.pallas_call(
        functools.partial(_prep_body, tiles_per_seq=seq // tb),
        grid=(t // tb,),
        in_specs=[row(D_SHIFT),
                  pl.BlockSpec((8, D_SHIFT), lambda i: (jnp.maximum(i * n8 - 1, 0), 0)),
                  pl.BlockSpec((8, D_SHIFT), lambda i: (jnp.minimum((i + 1) * n8, last8), 0))]
        + [_const_spec(c.shape) for c in consts],
        out_specs=[drow] * 6 + [row(D_RWKV),
                                pl.BlockSpec((N_DIR, CHUNKS_PER_TILE, 1, D_RWKV), lambda i: (0, i, 0, 0)),
                                row(D_RWKV), row(D_RWKV)],
        out_shape=[bf] * 6 + [jax.ShapeDtypeStruct((t, D_RWKV), BF16),
                              jax.ShapeDtypeStruct((N_DIR, t // CHUNK, 1, D_RWKV), F32),
                              jax.ShapeDtypeStruct((t, D_RWKV), F32),
                              jax.ShapeDtypeStruct((t, D_RWKV), F32)],
        compiler_params=_params(("parallel",)),
        name="rwkv_prep",
    )(ps, ps, ps, *consts)


def _scan_body(rt_ref, kt_ref, bt_ref, at_ref, be_ref, ke_ref, v_ref, pend_ref, y_ref, h_scr):
    d = pl.program_id(1)
    j = pl.program_id(2)

    @pl.when(j == 0)
    def _():
        h_scr[...] = jnp.zeros_like(h_scr)

    rowi = lax.broadcasted_iota(jnp.int32, (CHUNK, PAIR), 0)
    lane = lax.broadcasted_iota(jnp.int32, (CHUNK, PAIR), 1)
    diff = (rowi - (lane & (HEAD_DIM - 1))) * (1 - 2 * d)
    strict = diff > 0
    incl = diff >= 0
    first_half = lane < HEAD_DIM
    first_half_sq = lax.broadcasted_iota(jnp.int32, (PAIR, PAIR), 1) < HEAD_DIM
    eye_pair = (lax.broadcasted_iota(jnp.int32, (PAIR, PAIR), 0)
                == lax.broadcasted_iota(jnp.int32, (PAIR, PAIR), 1))
    eye = (lax.broadcasted_iota(jnp.int32, (CHUNK, CHUNK), 0)
           == lax.broadcasted_iota(jnp.int32, (CHUNK, CHUNK), 1)).astype(F32)
    zero_bf = jnp.zeros((CHUNK, PAIR), BF16)

    def chunk_body(ci, carry):
        c = jnp.where(d == 0, ci, CHUNKS_PER_TILE - 1 - ci)
        t0 = pl.multiple_of(c * CHUNK, CHUNK)
        for p in range(N_PAIR):
            ls = slice(p * PAIR, (p + 1) * PAIR)
            ld = lambda ref: ref[0, pl.ds(t0, CHUNK), ls]
            rt, kt, bt, at, be, ke = (ld(ref) for ref in (rt_ref, kt_ref, bt_ref, at_ref, be_ref, ke_ref))
            v = v_ref[pl.ds(t0, CHUNK), ls]
            pend = pend_ref[0, c, :, ls]
            hb = h_scr[p].astype(BF16)
            rhs_bk = jnp.concatenate([bt, kt], axis=0)
            zv = jnp.concatenate([zero_bf, v], axis=0)
            zv_wide = jnp.concatenate([zero_bf, v], axis=1)
            hq, hy0, hm, hd = [], [], [], []
            for hh in range(2):
                msk = first_half if hh == 0 else jnp.logical_not(first_half)
                sel = lambda a: jnp.where(msk, a, zero_bf)
                aa = lax.dot_general(jnp.concatenate([sel(at), sel(rt)], axis=0), rhs_bk, _NT,
                                     preferred_element_type=F32)
                aat = jnp.where(strict, aa[0:CHUNK], 0.0)
                aab = jnp.where(incl, aa[CHUNK:], 0.0)
                a = aat[:, 0:CHUNK]
                tinv = eye + a
                pk = a.astype(BF16)
                for _ in range(int(math.log2(CHUNK)) - 1):
                    pk = _dot(pk, pk).astype(BF16)
                    tinv = tinv + _dot(pk, tinv.astype(BF16))
                x1 = _dot(aat.astype(BF16), zv)
                z = _dot(tinv.astype(BF16), jnp.concatenate([at, x1.astype(BF16)], axis=1))
                r2 = jnp.concatenate([z.astype(BF16), zv_wide], axis=0)
                o = _dot(aab.astype(BF16), r2)
                md = lax.dot_general(jnp.concatenate([sel(be), sel(ke)], axis=0), r2, _TN,
                                     preferred_element_type=F32)
                hq.append(o[:, :PAIR])
                hy0.append(o[:, PAIR:])
                hm.append(md[:, :PAIR])
                hd.append(md[:, PAIR:])
            qh = rt.astype(F32) + jnp.where(first_half, hq[0], hq[1])
            y = _dot(qh.astype(BF16), hb) + jnp.where(first_half, hy0[0], hy0[1])
            y_ref[0, pl.ds(t0, CHUNK), ls] = y
            mbd = jnp.where(eye_pair, pend, 0.0) + jnp.where(first_half_sq, hm[0], hm[1])
            h_scr[p] = _dot(mbd.astype(BF16), hb) + jnp.where(first_half_sq, hd[0], hd[1])
        return carry

    lax.fori_loop(0, CHUNKS_PER_TILE, chunk_body, 0)


def _scan(rt, kt, bt, at, be, ke, v, pend, b, seq):
    t = v.shape[0]
    tc = TOK_TILE
    nblk = seq // tc
    tok = lambda bi, d, j: bi * nblk + jnp.where(d == 0, j, nblk - 1 - j)
    dblk = pl.BlockSpec((1, tc, D_RWKV), lambda bi, d, j: (d, tok(bi, d, j), 0))
    return pl.pallas_call(
        _scan_body,
        grid=(b, N_DIR, nblk),
        in_specs=[dblk] * 6
        + [pl.BlockSpec((tc, D_RWKV), lambda bi, d, j: (tok(bi, d, j), 0)),
           pl.BlockSpec((1, CHUNKS_PER_TILE, 1, D_RWKV), lambda bi, d, j: (d, tok(bi, d, j), 0, 0))],
        out_specs=dblk,
        out_shape=jax.ShapeDtypeStruct((N_DIR, t, D_RWKV), F32),
        scratch_shapes=[pltpu.VMEM((N_PAIR, PAIR, PAIR), F32)],
        compiler_params=_params(("parallel", "parallel", "arbitrary")),
        name="rwkv_scan",
    )(rt, kt, bt, at, be, ke, v, pend)


def _merge_body(y_ref, bonus_ref, g_ref, ona_ref, gt_ref, x_ref, seg_ref, lnw_ref, lnb_ref,
                wpa_ref, wpb_ref, wout_ref, npost_ref, o_ref):
    inv_n = 1.0 / HEAD_DIM
    y = y_ref[0] + y_ref[1]
    mean = _dot(y.astype(BF16), seg_ref[...]) * inv_n
    yc = y - mean
    var = _dot((yc * yc).astype(BF16), seg_ref[...]) * inv_n
    yn = yc * lax.rsqrt(var + GN_EPS) * lnw_ref[...] + lnb_ref[...]
    orw = ((yn + bonus_ref[...]) * g_ref[...]).astype(BF16)
    gates = gt_ref[...]
    merged = (gates[:, :D_MODEL] * _dot(ona_ref[...], wpa_ref[...])
              + gates[:, D_MODEL:] * _dot(orw, wpb_ref[...]))
    z = _dot(merged.astype(BF16), wout_ref[...])
    ms = jnp.mean(z * z, axis=-1, keepdims=True)
    o_ref[...] = x_ref[...] + z * lax.rsqrt(ms + RMS_EPS) * npost_ref[...]


def _merge(y, bonus, g, ona, gates, x, seg, lnw, lnb, wpa, wpb, wout, npost):
    t = x.shape[0]
    tm = TOK_TILE
    row = lambda w: pl.BlockSpec((tm, w), lambda i: (i, 0))
    consts = [seg, lnw, lnb, wpa, wpb, wout, npost]
    return pl.pallas_call(
        _merge_body,
        grid=(t // tm,),
        in_specs=[pl.BlockSpec((N_DIR, tm, D_RWKV), lambda i: (0, i, 0)),
                  row(D_RWKV), row(D_RWKV), row(D_NA), row(2 * D_MODEL), row(D_MODEL)]
        + [_const_spec(c.shape) for c in consts],
        out_specs=row(D_MODEL),
        out_shape=jax.ShapeDtypeStruct((t, D_MODEL), F32),
        compiler_params=_params(("parallel",)),
        name="merge",
    )(y, bonus, g, ona, gates, x, *consts)


def _ffn_body(x_ref, npre_ref, wup_ref, wdn_ref, npost_ref, o_ref, h_scr, acc_scr):
    jf = pl.program_id(1)

    @pl.when(jf == 0)
    def _():
        x = x_ref[...]
        ms = jnp.mean(x * x, axis=-1, keepdims=True)
        h_scr[...] = (x * lax.rsqrt(ms + RMS_EPS) * npre_ref[...]).astype(BF16)
        acc_scr[...] = jnp.zeros_like(acc_scr)

    u = jnp.maximum(_dot(h_scr[...], wup_ref[...]), 0.0)
    acc_scr[...] += _dot((u * u).astype(BF16), wdn_ref[...])

    @pl.when(jf == pl.num_programs(1) - 1)
    def _():
        f = acc_scr[...]
        ms = jnp.mean(f * f, axis=-1, keepdims=True)
        o_ref[...] = x_ref[...] + f * lax.rsqrt(ms + RMS_EPS) * npost_ref[...]


def _ffn(x, npre, wup, wdn, npost):
    t = x.shape[0]
    tm, tf = FFN_TOK_TILE, FFN_COL_TILE
    return pl.pallas_call(
        _ffn_body,
        grid=(t // tm, D_FF // tf),
        in_specs=[pl.BlockSpec((tm, D_MODEL), lambda i, jf: (i, 0)),
                  _const_spec(npre.shape),
                  pl.BlockSpec((D_MODEL, tf), lambda i, jf: (0, jf)),
                  pl.BlockSpec((tf, D_MODEL), lambda i, jf: (jf, 0)),
                  _const_spec(npost.shape)],
        out_specs=pl.BlockSpec((tm, D_MODEL), lambda i, jf: (i, 0)),
        out_shape=jax.ShapeDtypeStruct((t, D_MODEL), F32),
        scratch_shapes=[pltpu.VMEM((tm, D_MODEL), BF16), pltpu.VMEM((tm, D_MODEL), F32)],
        compiler_params=_params(("parallel", "arbitrary")),
        name="ffn",
    )(x, npre, wup, wdn, npost)


def _block_diag2(w):
    z = jnp.zeros_like(w[0])
    return jnp.concatenate([jnp.concatenate([w[0], z], axis=1),
                            jnp.concatenate([z, w[1]], axis=1)], axis=0)


def _layer_consts(w_in, b_gate, rpb, mu_prev, mu_next, w0, w2, a0, a2, g2, k_k, k_a, r_k,
                  ln_w, ln_b, w_proj_a, w_proj_b, w_out, w_up, w_down,
                  norm_pre_mix, norm_post_mix, norm_pre_ffn, norm_post_ffn):
    rowv = lambda a: a.reshape(1, -1).astype(F32)
    head = jnp.arange(D_RWKV) // HEAD_DIM
    idx = jnp.arange(CHUNK)
    tril = (idx[:, None] >= idx[None, :])
    return dict(
        wqkv=w_in[:, :3 * D_NA].astype(BF16),
        wsh=w_in[:, 3 * D_NA:3 * D_NA + D_SHIFT].astype(BF16),
        wgt=w_in[:, 3 * D_NA + D_SHIFT:].astype(BF16),
        bg=rowv(b_gate), bias=_na_bias_table(rpb),
        mup=rowv(mu_prev), mun=rowv(mu_next),
        w0=rowv(w0), w2bd=_block_diag2(w2).astype(BF16),
        a0=rowv(a0), a2bd=_block_diag2(a2).astype(BF16),
        g2=g2.astype(BF16), kkw=rowv(k_k), ka=rowv(k_a), rk=rowv(r_k),
        seg=(head[:, None] == head[None, :]).astype(BF16),
        tri=jnp.stack([tril, tril.T]).astype(BF16),
        lnw=rowv(ln_w), lnb=rowv(ln_b),
        wpa=w_proj_a.astype(BF16), wpb=w_proj_b.astype(BF16), wout=w_out.astype(BF16),
        wup=w_up.astype(BF16), wdn=w_down.astype(BF16),
        npre_mix=rowv(norm_pre_mix), npost_mix=rowv(norm_post_mix),
        npre_ffn=rowv(norm_pre_ffn), npost_ffn=rowv(norm_post_ffn),
    )


def _encoder_layer(x, b, seq, c):
    q, k, v, ps, gates = _inproj(x, c["npre_mix"], c["wqkv"], c["wsh"], c["wgt"], c["bg"])
    ona = _natten(q, k, v, c["bias"], b, seq)
    rt, kt, bt, at, be, ke, vr, pend, g, bonus = _prep(
        ps, seq, c["mup"], c["mun"], c["w0"], c["w2bd"], c["a0"], c["a2bd"], c["g2"],
        c["kkw"], c["ka"], c["rk"], c["seg"], c["tri"])
    y = _scan(rt, kt, bt, at, be, ke, vr, pend, b, seq)
    x1 = _merge(y, bonus, g, ona, gates, x, c["seg"], c["lnw"], c["lnb"],
                c["wpa"], c["wpb"], c["wout"], c["npost_mix"])
    return _ffn(x1, c["npre_ffn"], c["wup"], c["wdn"], c["npost_ffn"])


def _trunk(x, layer_consts):
    b, seq, dm = x.shape
    assert dm == D_MODEL and seq % FFN_TOK_TILE == 0 and seq // GRID_W >= WIN_H
    h = x.reshape(b * seq, dm)
    for c in layer_consts:
        h = _encoder_layer(h, b, seq, c)
    return h.reshape(b, seq, dm)


def kernel(x_prompt, x_sample, w_in, b_gate, rpb, mu_prev, mu_next, w0, w2, a0, a2, g2, k_k, k_a, r_k, ln_w, ln_b, w_proj_a, w_proj_b, w_out, w_up, w_down, norm_pre_mix, norm_post_mix, norm_pre_ffn, norm_post_ffn):
    stacked = (w_in, b_gate, rpb, mu_prev, mu_next, w0, w2, a0, a2, g2, k_k, k_a, r_k,
               ln_w, ln_b, w_proj_a, w_proj_b, w_out, w_up, w_down,
               norm_pre_mix, norm_post_mix, norm_pre_ffn, norm_post_ffn)
    depth = w_in.shape[0]
    layer_consts = [_layer_consts(*[p[i] for p in stacked]) for i in range(depth)]
    return (_trunk(x_prompt, layer_consts), _trunk(x_sample, layer_consts))
```

```python
import functools
import math

import jax
import jax.numpy as jnp
from jax import lax
from jax.experimental import pallas as pl
from jax.experimental.pallas import tpu as pltpu

F32 = jnp.float32
BF16 = jnp.bfloat16

D_MODEL = 1024
GRID_W = 64
WIN_H = 8
WIN_W = 16
NA_HEADS = 8
HEAD_DIM = 64
D_NA = NA_HEADS * HEAD_DIM
RWKV_HEADS = 8
D_RWKV = RWKV_HEADS * HEAD_DIM
DECAY_LORA = 64
ICLR_LORA = 64
GATE_LORA = 128
N_DIR = 2
D_FF = 4 * D_MODEL
D_SHIFT = 3 * D_RWKV + N_DIR * DECAY_LORA + N_DIR * ICLR_LORA + GATE_LORA
RMS_EPS = 1e-6
GN_EPS = 64e-5
KK_EPS = 1e-12
NEG_BIG = -1e30

CHUNK = 64
PAIR = 2 * HEAD_DIM
N_PAIR = D_RWKV // PAIR
TOK_TILE = 512
ROWS_PER_TILE = TOK_TILE // GRID_W
CHUNKS_PER_TILE = TOK_TILE // CHUNK
NA_ROWS_PER_ITER = 2
SCAN_GROUP = 2
FFN_TOK_TILE = 1024
FFN_COL_TILE = 512
V7X_VMEM_LIMIT = 56 * 1024 * 1024

_NT = (((1,), (1,)), ((), ()))
_TN = (((0,), (0,)), ((), ()))


def _dot(a, b):
    return jnp.dot(a, b, preferred_element_type=F32)


def _sigmoid(x):
    return 1.0 / (1.0 + jnp.exp(-x))


def _params(sem):
    return pltpu.CompilerParams(dimension_semantics=sem, vmem_limit_bytes=V7X_VMEM_LIMIT)


def _const_spec(shape):
    nd = len(shape)
    return pl.BlockSpec(shape, lambda *_: (0,) * nd)


def _inproj_body(x_ref, g_ref, wqkv_ref, wsh_ref, wgt_ref, bg_ref,
                 q_ref, k_ref, v_ref, ps_ref, gt_ref):
    x = x_ref[...]
    ms = jnp.mean(x * x, axis=-1, keepdims=True)
    h = (x * lax.rsqrt(ms + RMS_EPS) * g_ref[...]).astype(BF16)
    qkv = _dot(h, wqkv_ref[...])
    q_ref[...] = (qkv[:, 0:D_NA] * (HEAD_DIM ** -0.5)).astype(BF16)
    k_ref[...] = qkv[:, D_NA:2 * D_NA].astype(BF16)
    v_ref[...] = qkv[:, 2 * D_NA:3 * D_NA].astype(BF16)
    ps_ref[...] = _dot(h, wsh_ref[...])
    gt_ref[...] = _sigmoid(_dot(h, wgt_ref[...]) + bg_ref[...])


def _inproj(x, g, wqkv, wsh, wgt, bg):
    t = x.shape[0]
    tm = TOK_TILE
    row = lambda w: pl.BlockSpec((tm, w), lambda i: (i, 0))
    return pl.pallas_call(
        _inproj_body,
        grid=(t // tm,),
        in_specs=[row(D_MODEL), _const_spec(g.shape), _const_spec(wqkv.shape),
                  _const_spec(wsh.shape), _const_spec(wgt.shape), _const_spec(bg.shape)],
        out_specs=[row(D_NA), row(D_NA), row(D_NA), row(D_SHIFT), row(2 * D_MODEL)],
        out_shape=[jax.ShapeDtypeStruct((t, D_NA), BF16)] * 3
        + [jax.ShapeDtypeStruct((t, D_SHIFT), F32), jax.ShapeDtypeStruct((t, 2 * D_MODEL), F32)],
        compiler_params=_params(("parallel",)),
        name="inproj",
    )(x, g, wqkv, wsh, wgt, bg)


def _na_bias_table(rpb):
    c = jnp.arange(GRID_W)
    col_start = jnp.clip(c - WIN_W // 2, 0, GRID_W - WIN_W)
    col_ok = (c[None, :] >= col_start[:, None]) & (c[None, :] < col_start[:, None] + WIN_W)
    col_idx = jnp.clip(c[None, :] - c[:, None] + (WIN_W - 1), 0, 2 * WIN_W - 2)
    onehot = (col_idx[:, :, None] == jnp.arange(2 * WIN_W - 1)[None, None, :]).astype(F32)
    cols = jnp.einsum("hdc,qkc->hqdk", rpb.astype(F32), onehot,
                      precision=lax.Precision.HIGHEST)
    cols = jnp.where(col_ok[None, :, None, :], cols, NEG_BIG)
    tab = jnp.stack([cols[:, :, WIN_H - 1 - e:2 * WIN_H - 1 - e, :] for e in range(WIN_H)])
    return tab.reshape(WIN_H, NA_HEADS, GRID_W, WIN_H * GRID_W)


def _natten_body(q_ref, kp_ref, kc_ref, kn_ref, vp_ref, vc_ref, vn_ref, bias_ref, o_ref,
                 kwin, vwin, *, rows):
    j = pl.program_id(1)
    kwin[0:TOK_TILE, :] = kp_ref[...]
    kwin[TOK_TILE:2 * TOK_TILE, :] = kc_ref[...]
    kwin[2 * TOK_TILE:3 * TOK_TILE, :] = kn_ref[...]
    vwin[0:TOK_TILE, :] = vp_ref[...]
    vwin[TOK_TILE:2 * TOK_TILE, :] = vc_ref[...]
    vwin[2 * TOK_TILE:3 * TOK_TILE, :] = vn_ref[...]
    first_half = lax.broadcasted_iota(jnp.int32, (GRID_W, PAIR), 1) < HEAD_DIM
    n_keys = WIN_H * GRID_W
    zero_q = jnp.zeros((GRID_W, PAIR), BF16)

    def rows_body(it, carry):
        work = []
        for rr in range(NA_ROWS_PER_ITER):
            jr = it * NA_ROWS_PER_ITER + rr
            r = j * ROWS_PER_TILE + jr
            rs = jnp.clip(r - WIN_H // 2, 0, rows - WIN_H)
            off = pl.multiple_of((rs - (j - 1) * ROWS_PER_TILE) * GRID_W, GRID_W)
            q0 = pl.multiple_of(jr * GRID_W, GRID_W)
            for p in range(N_PAIR):
                work.append((r - rs, off, q0, p, slice(p * PAIR, (p + 1) * PAIR)))
        scores = []
        for e, off, q0, p, ls in work:
            qp = q_ref[pl.ds(q0, GRID_W), ls]
            q2 = jnp.concatenate([jnp.where(first_half, qp, zero_q),
                                  jnp.where(first_half, zero_q, qp)], axis=0)
            scores.append(lax.dot_general(q2, kwin[pl.ds(off, n_keys), ls], _NT,
                                          preferred_element_type=F32))
        probs, sums = [], []
        for (e, off, q0, p, ls), s2 in zip(work, scores):
            pr2 = []
            for hh in range(2):
                s = s2[hh * GRID_W:(hh + 1) * GRID_W] + bias_ref[e, 2 * p + hh]
                pr = jnp.exp(s - jnp.max(s, axis=-1, keepdims=True))
                sums.append(jnp.sum(pr, axis=-1, keepdims=True))
                pr2.append(pr.astype(BF16))
            probs.append(jnp.concatenate(pr2, axis=0))
        outs = [_dot(pr2, vwin[pl.ds(off, n_keys), ls]) for (e, off, q0, p, ls), pr2 in zip(work, probs)]
        for i, ((e, off, q0, p, ls), o2) in enumerate(zip(work, outs)):
            o0 = o2[0:GRID_W] / sums[2 * i]
            o1 = o2[GRID_W:] / sums[2 * i + 1]
            o_ref[pl.ds(q0, GRID_W), ls] = jnp.where(first_half, o0, o1).astype(BF16)
        return carry

    lax.fori_loop(0, ROWS_PER_TILE // NA_ROWS_PER_ITER, rows_body, 0)


def _natten(q, k, v, bias, b, seq):
    t = q.shape[0]
    rows = seq // GRID_W
    nblk = seq // TOK_TILE
    cur = lambda bi, j: (bi * nblk + j, 0)
    prev = lambda bi, j: (bi * nblk + jnp.maximum(j - 1, 0), 0)
    nxt = lambda bi, j: (bi * nblk + jnp.minimum(j + 1, nblk - 1), 0)
    blk = lambda im: pl.BlockSpec((TOK_TILE, D_NA), im)
    return pl.pallas_call(
        functools.partial(_natten_body, rows=rows),
        grid=(b, nblk),
        in_specs=[blk(cur), blk(prev), blk(cur), blk(nxt), blk(prev), blk(cur), blk(nxt),
                  _const_spec(bias.shape)],
        out_specs=blk(cur),
        out_shape=jax.ShapeDtypeStruct((t, D_NA), BF16),
        scratch_shapes=[pltpu.VMEM((3 * TOK_TILE, D_NA), BF16),
                        pltpu.VMEM((3 * TOK_TILE, D_NA), BF16)],
        compiler_params=_params(("parallel", "parallel")),
        name="natten",
    )(q, k, k, k, v, v, v, bias)


def _split3_dot(tri, x):
    x0 = x.astype(BF16)
    r1 = x - x0.astype(F32)
    x1 = r1.astype(BF16)
    x2 = (r1 - x1.astype(F32)).astype(BF16)
    return _dot(tri, x0) + _dot(tri, x1) + _dot(tri, x2)


def _prep_body(ps_ref, pp_ref, pn_ref, mup_ref, mun_ref, w0_ref, w2_ref, a0_ref, a2_ref, g2_ref,
               kkw_ref, ka_ref, rk_ref, seg_ref, tri_ref,
               rt_ref, kt_ref, bt_ref, at_ref, be_ref, ke_ref, v_ref, pend_ref, g_ref, bonus_ref,
               *, tiles_per_seq):
    i = pl.program_id(0)
    tb = TOK_TILE
    pos = lax.rem(i, tiles_per_seq)
    p = ps_ref[...]
    prev_row = jnp.where(pos == 0, 0.0, pp_ref[7:8, :])
    next_row = jnp.where(pos == tiles_per_seq - 1, 0.0, pn_ref[0:1, :])
    rowi = lax.broadcasted_iota(jnp.int32, (tb, 1), 0)
    prev = jnp.where(rowi == 0, prev_row, pltpu.roll(p, 1, 0))
    nxt = jnp.where(rowi == tb - 1, next_row, pltpu.roll(p, tb - 1, 0))
    sh = p + mup_ref[...] * (prev - p) + mun_ref[...] * (nxt - p)

    r = sh[:, 0:D_RWKV]
    k = sh[:, D_RWKV:2 * D_RWKV]
    v = sh[:, 2 * D_RWKV:3 * D_RWKV]
    off = 3 * D_RWKV
    wl = sh[:, off:off + 2 * DECAY_LORA]
    al = sh[:, off + 2 * DECAY_LORA:off + 2 * DECAY_LORA + 2 * ICLR_LORA]
    gl = sh[:, off + 2 * DECAY_LORA + 2 * ICLR_LORA:]

    zw = w0_ref[...] + _dot(jnp.tanh(wl).astype(BF16), w2_ref[...])
    za = a0_ref[...] + _dot(al.astype(BF16), a2_ref[...])
    g_ref[...] = _dot(_sigmoid(gl).astype(BF16), g2_ref[...])
    lw = (-math.exp(-0.5)) * _sigmoid(zw)
    iclr = _sigmoid(za)

    kkv = k * kkw_ref[...]
    kkn = kkv * lax.rsqrt(_dot((kkv * kkv).astype(BF16), seg_ref[...]) + KK_EPS)
    v_ref[...] = v.astype(BF16)

    bonus_acc = jnp.zeros_like(r)
    for d in range(N_DIR):
        ds_ = slice(d * D_RWKV, (d + 1) * D_RWKV)
        icl = iclr[:, ds_]
        kd = k * (1.0 + (icl - 1.0) * ka_ref[...])
        bd = kkn * icl
        bonus_acc = bonus_acc + r * kd * rk_ref[...]
        lwd = lw[:, ds_]
        tri = tri_ref[d]
        for c in range(CHUNKS_PER_TILE):
            cs = slice(c * CHUNK, (c + 1) * CHUNK)
            x = lwd[cs]
            lp = _split3_dot(tri, x)
            lend = lp[CHUNK - 1:CHUNK] if d == 0 else lp[0:1]
            pw = jnp.exp(lp)
            pinv = jnp.exp(-lp)
            pex = jnp.exp(lp - x)
            pe = jnp.exp(lend - lp)
            rt_ref[d, cs, :] = (r[cs] * pw).astype(BF16)
            kt_ref[d, cs, :] = (kd[cs] * pinv).astype(BF16)
            bt_ref[d, cs, :] = (bd[cs] * pinv).astype(BF16)
            at_ref[d, cs, :] = (-kkn[cs] * pex).astype(BF16)
            be_ref[d, cs, :] = (bd[cs] * pe).astype(BF16)
            ke_ref[d, cs, :] = (kd[cs] * pe).astype(BF16)
            pend_ref[d, c] = jnp.exp(lend)
    bonus_ref[...] = _dot(bonus_acc.astype(BF16), seg_ref[...]) * v


def _prep(ps, seq, mup, mun, w0, w2bd, a0, a2bd, g2, kkw, ka, rk, seg, tri):
    t = ps.shape[0]
    tb = TOK_TILE
    n8 = tb // 8
    last8 = t // 8 - 1
    row = lambda w: pl.BlockSpec((tb, w), lambda i: (i, 0))
    drow = pl.BlockSpec((N_DIR, tb, D_RWKV), lambda i: (0, i, 0))
    consts = [mup, mun, w0, w2bd, a0, a2bd, g2, kkw, ka, rk, seg, tri]
    bf = jax.ShapeDtypeStruct((N_DIR, t, D_RWKV), BF16)
    return pl.pallas_call(
        functools.partial(_prep_body, tiles_per_seq=seq // tb),
        grid=(t // tb,),
        in_specs=[row(D_SHIFT),
                  pl.BlockSpec((8, D_SHIFT), lambda i: (jnp.maximum(i * n8 - 1, 0), 0)),
                  pl.BlockSpec((8, D_SHIFT), lambda i: (jnp.minimum((i + 1) * n8, last8), 0))]
        + [_const_spec(c.shape) for c in consts],
        out_specs=[drow] * 6 + [row(D_RWKV),
                                pl.BlockSpec((N_DIR, CHUNKS_PER_TILE, 1, D_RWKV), lambda i: (0, i, 0, 0)),
                                row(D_RWKV), row(D_RWKV)],
        out_shape=[bf] * 6 + [jax.ShapeDtypeStruct((t, D_RWKV), BF16),
                              jax.ShapeDtypeStruct((N_DIR, t // CHUNK, 1, D_RWKV), F32),
                              jax.ShapeDtypeStruct((t, D_RWKV), F32),
                              jax.ShapeDtypeStruct((t, D_RWKV), F32)],
        compiler_params=_params(("parallel",)),
        name="rwkv_prep",
    )(ps, ps, ps, *consts)


def _scan_body(rt_ref, kt_ref, bt_ref, at_ref, be_ref, ke_ref, v_ref, pend_ref, y_ref,
               h_scr, q_scr, m_scr, d_scr):
    d = pl.program_id(1)
    j = pl.program_id(2)

    @pl.when(j == 0)
    def _():
        h_scr[...] = jnp.zeros_like(h_scr)

    rowi = lax.broadcasted_iota(jnp.int32, (CHUNK, PAIR), 0)
    lane = lax.broadcasted_iota(jnp.int32, (CHUNK, PAIR), 1)
    diff = (rowi - (lane & (HEAD_DIM - 1))) * (1 - 2 * d)
    strict = diff > 0
    incl = diff >= 0
    first_half = lane < HEAD_DIM
    eye_hi = (lane - HEAD_DIM == rowi).astype(F32)
    first_half_sq = lax.broadcasted_iota(jnp.int32, (PAIR, PAIR), 1) < HEAD_DIM
    eye_pair = (lax.broadcasted_iota(jnp.int32, (PAIR, PAIR), 0)
                == lax.broadcasted_iota(jnp.int32, (PAIR, PAIR), 1))
    zero_bf = jnp.zeros((CHUNK, PAIR), BF16)
    zero_wide = jnp.zeros((CHUNK, 2 * PAIR), BF16)
    n_sq = int(math.log2(CHUNK))

    def group_body(gi, carry):
        pairs = []
        for cc in range(SCAN_GROUP):
            c = gi * SCAN_GROUP + cc
            t0 = pl.multiple_of(c * CHUNK, CHUNK)
            for p in range(N_PAIR):
                ls = slice(p * PAIR, (p + 1) * PAIR)
                ld = lambda ref: ref[0, pl.ds(t0, CHUNK), ls]
                pairs.append(dict(c=c, t0=t0, p=p, ls=ls, rt=ld(rt_ref), kt=ld(kt_ref), bt=ld(bt_ref),
                                  at=ld(at_ref), be=ld(be_ref), ke=ld(ke_ref),
                                  v=v_ref[pl.ds(t0, CHUNK), ls], pend=pend_ref[0, c, :, ls]))
        sel = lambda a, hh: jnp.where(first_half, a, zero_bf) if hh == 0 else jnp.where(first_half, zero_bf, a)
        for pr in pairs:
            lhs = jnp.concatenate([sel(pr["at"], 0), sel(pr["rt"], 0), sel(pr["at"], 1), sel(pr["rt"], 1)],
                                  axis=0)
            pr["aa"] = lax.dot_general(lhs, jnp.concatenate([pr["bt"], pr["kt"]], axis=0), _NT,
                                       preferred_element_type=F32)
        for pr in pairs:
            pr["aat"], pr["aab"], pr["x"] = [], [], []
            for hh in range(2):
                blk = pr["aa"][hh * 2 * CHUNK:(hh + 1) * 2 * CHUNK]
                aat = jnp.where(strict, blk[0:CHUNK], 0.0)
                pr["aat"].append(aat.astype(BF16))
                pr["aab"].append(jnp.where(incl, blk[CHUNK:], 0.0).astype(BF16))
                pr["x"].append(jnp.where(first_half, aat, eye_hi))
        for pr in pairs:
            zv = jnp.concatenate([zero_bf, pr["v"]], axis=0)
            pr["x1"] = _dot(jnp.concatenate(pr["aat"], axis=0), zv).astype(BF16)
        for k in range(n_sq):
            res = []
            for pr in pairs:
                for hh in range(2):
                    xb = pr["x"][hh].astype(BF16)
                    res.append(_dot(xb[:, 0:CHUNK], xb))
            it = iter(res)
            for pr in pairs:
                for hh in range(2):
                    r = next(it)
                    pr["x"][hh] = jnp.where(first_half, r, pr["x"][hh] + r)
        for pr in pairs:
            pr["r2"] = []
            for hh in range(2):
                w = jnp.concatenate([pr["at"], pr["x1"][hh * CHUNK:(hh + 1) * CHUNK]], axis=1)
                pr["r2"].append(_dot(pr["x"][hh].astype(BF16), jnp.concatenate([zero_wide, w], axis=0)))
        for pr in pairs:
            zv_wide = jnp.concatenate([zero_bf, pr["v"]], axis=1)
            pr["r2"] = [jnp.concatenate([z.astype(BF16), zv_wide], axis=0) for z in pr["r2"]]
        for pr in pairs:
            pr["o"] = [_dot(pr["aab"][hh], pr["r2"][hh]) for hh in range(2)]
            pr["md"] = [lax.dot_general(jnp.concatenate([sel(pr["be"], hh), sel(pr["ke"], hh)], axis=0),
                                        pr["r2"][hh], _TN, preferred_element_type=F32) for hh in range(2)]
        for pr in pairs:
            o, md, t0, ls = pr["o"], pr["md"], pr["t0"], pr["ls"]
            qh = pr["rt"].astype(F32) + jnp.where(first_half, o[0][:, :PAIR], o[1][:, :PAIR])
            q_scr[pl.ds(t0, CHUNK), ls] = qh.astype(BF16)
            y_ref[0, pl.ds(t0, CHUNK), ls] = jnp.where(first_half, o[0][:, PAIR:], o[1][:, PAIR:])
            mbd = (jnp.where(eye_pair, pr["pend"], 0.0)
                   + jnp.where(first_half_sq, md[0][:, :PAIR], md[1][:, :PAIR]))
            m_scr[pr["c"], pr["p"]] = mbd.astype(BF16)
            d_scr[pr["c"], pr["p"]] = jnp.where(first_half_sq, md[0][:, PAIR:], md[1][:, PAIR:])
        return carry

    lax.fori_loop(0, CHUNKS_PER_TILE // SCAN_GROUP, group_body, 0)

    def chunk_body(ci, carry):
        c = jnp.where(d == 0, ci, CHUNKS_PER_TILE - 1 - ci)
        t0 = pl.multiple_of(c * CHUNK, CHUNK)
        hbs = [h_scr[p].astype(BF16) for p in range(N_PAIR)]
        ys = [_dot(q_scr[pl.ds(t0, CHUNK), p * PAIR:(p + 1) * PAIR], hbs[p]) for p in range(N_PAIR)]
        hs = [_dot(m_scr[c, p], hbs[p]) for p in range(N_PAIR)]
        for p in range(N_PAIR):
            ls = slice(p * PAIR, (p + 1) * PAIR)
            y_ref[0, pl.ds(t0, CHUNK), ls] = y_ref[0, pl.ds(t0, CHUNK), ls] + ys[p]
            h_scr[p] = hs[p] + d_scr[c, p]
        return carry

    lax.fori_loop(0, CHUNKS_PER_TILE, chunk_body, 0)


def _scan(rt, kt, bt, at, be, ke, v, pend, b, seq):
    t = v.shape[0]
    tc = TOK_TILE
    nblk = seq // tc
    tok = lambda bi, d, j: bi * nblk + jnp.where(d == 0, j, nblk - 1 - j)
    dblk = pl.BlockSpec((1, tc, D_RWKV), lambda bi, d, j: (d, tok(bi, d, j), 0))
    return pl.pallas_call(
        _scan_body,
        grid=(b, N_DIR, nblk),
        in_specs=[dblk] * 6
        + [pl.BlockSpec((tc, D_RWKV), lambda bi, d, j: (tok(bi, d, j), 0)),
           pl.BlockSpec((1, CHUNKS_PER_TILE, 1, D_RWKV), lambda bi, d, j: (d, tok(bi, d, j), 0, 0))],
        out_specs=dblk,
        out_shape=jax.ShapeDtypeStruct((N_DIR, t, D_RWKV), F32),
        scratch_shapes=[pltpu.VMEM((N_PAIR, PAIR, PAIR), F32),
                        pltpu.VMEM((tc, D_RWKV), BF16),
                        pltpu.VMEM((CHUNKS_PER_TILE, N_PAIR, PAIR, PAIR), BF16),
                        pltpu.VMEM((CHUNKS_PER_TILE, N_PAIR, PAIR, PAIR), F32)],
        compiler_params=_params(("parallel", "parallel", "arbitrary")),
        name="rwkv_scan",
    )(rt, kt, bt, at, be, ke, v, pend)


def _merge_body(y_ref, bonus_ref, g_ref, ona_ref, gt_ref, x_ref, seg_ref, lnw_ref, lnb_ref,
                wpa_ref, wpb_ref, wout_ref, npost_ref, o_ref):
    inv_n = 1.0 / HEAD_DIM
    y = y_ref[0] + y_ref[1]
    mean = _dot(y.astype(BF16), seg_ref[...]) * inv_n
    yc = y - mean
    var = _dot((yc * yc).astype(BF16), seg_ref[...]) * inv_n
    yn = yc * lax.rsqrt(var + GN_EPS) * lnw_ref[...] + lnb_ref[...]
    orw = ((yn + bonus_ref[...]) * g_ref[...]).astype(BF16)
    gates = gt_ref[...]
    merged = (gates[:, :D_MODEL] * _dot(ona_ref[...], wpa_ref[...])
              + gates[:, D_MODEL:] * _dot(orw, wpb_ref[...]))
    z = _dot(merged.astype(BF16), wout_ref[...])
    ms = jnp.mean(z * z, axis=-1, keepdims=True)
    o_ref[...] = x_ref[...] + z * lax.rsqrt(ms + RMS_EPS) * npost_ref[...]


def _merge(y, bonus, g, ona, gates, x, seg, lnw, lnb, wpa, wpb, wout, npost):
    t = x.shape[0]
    tm = TOK_TILE
    row = lambda w: pl.BlockSpec((tm, w), lambda i: (i, 0))
    consts = [seg, lnw, lnb, wpa, wpb, wout, npost]
    return pl.pallas_call(
        _merge_body,
        grid=(t // tm,),
        in_specs=[pl.BlockSpec((N_DIR, tm, D_RWKV), lambda i: (0, i, 0)),
                  row(D_RWKV), row(D_RWKV), row(D_NA), row(2 * D_MODEL), row(D_MODEL)]
        + [_const_spec(c.shape) for c in consts],
        out_specs=row(D_MODEL),
        out_shape=jax.ShapeDtypeStruct((t, D_MODEL), F32),
        compiler_params=_params(("parallel",)),
        name="merge",
    )(y, bonus, g, ona, gates, x, *consts)


def _ffn_body(x_ref, npre_ref, wup_ref, wdn_ref, npost_ref, o_ref, h_scr, acc_scr):
    jf = pl.program_id(1)

    @pl.when(jf == 0)
    def _():
        x = x_ref[...]
        ms = jnp.mean(x * x, axis=-1, keepdims=True)
        h_scr[...] = (x * lax.rsqrt(ms + RMS_EPS) * npre_ref[...]).astype(BF16)
        acc_scr[...] = jnp.zeros_like(acc_scr)

    u = jnp.maximum(_dot(h_scr[...], wup_ref[...]), 0.0)
    acc_scr[...] += _dot((u * u).astype(BF16), wdn_ref[...])

    @pl.when(jf == pl.num_programs(1) - 1)
    def _():
        f = acc_scr[...]
        ms = jnp.mean(f * f, axis=-1, keepdims=True)
        o_ref[...] = x_ref[...] + f * lax.rsqrt(ms + RMS_EPS) * npost_ref[...]


def _ffn(x, npre, wup, wdn, npost):
    t = x.shape[0]
    tm, tf = FFN_TOK_TILE, FFN_COL_TILE
    return pl.pallas_call(
        _ffn_body,
        grid=(t // tm, D_FF // tf),
        in_specs=[pl.BlockSpec((tm, D_MODEL), lambda i, jf: (i, 0)),
                  _const_spec(npre.shape),
                  pl.BlockSpec((D_MODEL, tf), lambda i, jf: (0, jf)),
                  pl.BlockSpec((tf, D_MODEL), lambda i, jf: (jf, 0)),
                  _const_spec(npost.shape)],
        out_specs=pl.BlockSpec((tm, D_MODEL), lambda i, jf: (i, 0)),
        out_shape=jax.ShapeDtypeStruct((t, D_MODEL), F32),
        scratch_shapes=[pltpu.VMEM((tm, D_MODEL), BF16), pltpu.VMEM((tm, D_MODEL), F32)],
        compiler_params=_params(("parallel", "arbitrary")),
        name="ffn",
    )(x, npre, wup, wdn, npost)


def _block_diag2(w):
    z = jnp.zeros_like(w[0])
    return jnp.concatenate([jnp.concatenate([w[0], z], axis=1),
                            jnp.concatenate([z, w[1]], axis=1)], axis=0)


def _layer_consts(w_in, b_gate, rpb, mu_prev, mu_next, w0, w2, a0, a2, g2, k_k, k_a, r_k,
                  ln_w, ln_b, w_proj_a, w_proj_b, w_out, w_up, w_down,
                  norm_pre_mix, norm_post_mix, norm_pre_ffn, norm_post_ffn):
    rowv = lambda a: a.reshape(1, -1).astype(F32)
    head = jnp.arange(D_RWKV) // HEAD_DIM
    idx = jnp.arange(CHUNK)
    tril = (idx[:, None] >= idx[None, :])
    return dict(
        wqkv=w_in[:, :3 * D_NA].astype(BF16),
        wsh=w_in[:, 3 * D_NA:3 * D_NA + D_SHIFT].astype(BF16),
        wgt=w_in[:, 3 * D_NA + D_SHIFT:].astype(BF16),
        bg=rowv(b_gate), bias=_na_bias_table(rpb),
        mup=rowv(mu_prev), mun=rowv(mu_next),
        w0=rowv(w0), w2bd=_block_diag2(w2).astype(BF16),
        a0=rowv(a0), a2bd=_block_diag2(a2).astype(BF16),
        g2=g2.astype(BF16), kkw=rowv(k_k), ka=rowv(k_a), rk=rowv(r_k),
        seg=(head[:, None] == head[None, :]).astype(BF16),
        tri=jnp.stack([tril, tril.T]).astype(BF16),
        lnw=rowv(ln_w), lnb=rowv(ln_b),
        wpa=w_proj_a.astype(BF16), wpb=w_proj_b.astype(BF16), wout=w_out.astype(BF16),
        wup=w_up.astype(BF16), wdn=w_down.astype(BF16),
        npre_mix=rowv(norm_pre_mix), npost_mix=rowv(norm_post_mix),
        npre_ffn=rowv(norm_pre_ffn), npost_ffn=rowv(norm_post_ffn),
    )


def _encoder_layer(x, b, seq, c):
    q, k, v, ps, gates = _inproj(x, c["npre_mix"], c["wqkv"], c["wsh"], c["wgt"], c["bg"])
    ona = _natten(q, k, v, c["bias"], b, seq)
    rt, kt, bt, at, be, ke, vr, pend, g, bonus = _prep(
        ps, seq, c["mup"], c["mun"], c["w0"], c["w2bd"], c["a0"], c["a2bd"], c["g2"],
        c["kkw"], c["ka"], c["rk"], c["seg"], c["tri"])
    y = _scan(rt, kt, bt, at, be, ke, vr, pend, b, seq)
    x1 = _merge(y, bonus, g, ona, gates, x, c["seg"], c["lnw"], c["lnb"],
                c["wpa"], c["wpb"], c["wout"], c["npost_mix"])
    return _ffn(x1, c["npre_ffn"], c["wup"], c["wdn"], c["npost_ffn"])


def _trunk(x, layer_consts):
    b, seq, dm = x.shape
    assert dm == D_MODEL and seq % FFN_TOK_TILE == 0 and seq // GRID_W >= WIN_H
    h = x.reshape(b * seq, dm)
    for c in layer_consts:
        h = _encoder_layer(h, b, seq, c)
    return h.reshape(b, seq, dm)


def kernel(x_prompt, x_sample, w_in, b_gate, rpb, mu_prev, mu_next, w0, w2, a0, a2, g2, k_k, k_a, r_k, ln_w, ln_b, w_proj_a, w_proj_b, w_out, w_up, w_down, norm_pre_mix, norm_post_mix, norm_pre_ffn, norm_post_ffn):
    stacked = (w_in, b_gate, rpb, mu_prev, mu_next, w0, w2, a0, a2, g2, k_k, k_a, r_k,
               ln_w, ln_b, w_proj_a, w_proj_b, w_out, w_up, w_down,
               norm_pre_mix, norm_post_mix, norm_pre_ffn, norm_post_ffn)
    depth = w_in.shape[0]
    layer_consts = [_layer_consts(*[p[i] for p in stacked]) for i in range(depth)]
    return (_trunk(x_prompt, layer_consts), _trunk(x_sample, layer_consts))
```

```python
import functools
import math

import jax
import jax.numpy as jnp
from jax import lax
from jax.experimental import pallas as pl
from jax.experimental.pallas import tpu as pltpu

F32 = jnp.float32
BF16 = jnp.bfloat16

D_MODEL = 1024
GRID_W = 64
WIN_H = 8
WIN_W = 16
NA_HEADS = 8
HEAD_DIM = 64
D_NA = NA_HEADS * HEAD_DIM
RWKV_HEADS = 8
D_RWKV = RWKV_HEADS * HEAD_DIM
DECAY_LORA = 64
ICLR_LORA = 64
GATE_LORA = 128
N_DIR = 2
D_FF = 4 * D_MODEL
D_SHIFT = 3 * D_RWKV + N_DIR * DECAY_LORA + N_DIR * ICLR_LORA + GATE_LORA
RMS_EPS = 1e-6
GN_EPS = 64e-5
KK_EPS = 1e-12
NEG_BIG = -1e30

CHUNK = 64
PAIR = 2 * HEAD_DIM
N_PAIR = D_RWKV // PAIR
TOK_TILE = 512
ROWS_PER_TILE = TOK_TILE // GRID_W
CHUNKS_PER_TILE = TOK_TILE // CHUNK
NA_ROWS_PER_ITER = 2
SCAN_GROUP = 4
V7X_VMEM_LIMIT = 56 * 1024 * 1024

_NT = (((1,), (1,)), ((), ()))


def _dot(a, b):
    return jnp.dot(a, b, preferred_element_type=F32)


def _sigmoid(x):
    return 1.0 / (1.0 + jnp.exp(-x))


def _params(sem):
    return pltpu.CompilerParams(dimension_semantics=sem, vmem_limit_bytes=V7X_VMEM_LIMIT)


def _const_spec(shape):
    nd = len(shape)
    return pl.BlockSpec(shape, lambda *_: (0,) * nd, pipeline_mode=pl.Buffered(1))


def _inproj_body(x_ref, g_ref, wqkv_ref, wsh_ref, wgt_ref, bg_ref,
                 q_ref, k_ref, v_ref, ps_ref, gt_ref):
    x = x_ref[...]
    ms = jnp.mean(x * x, axis=-1, keepdims=True)
    h = (x * lax.rsqrt(ms + RMS_EPS) * g_ref[...]).astype(BF16)
    qkv = _dot(h, wqkv_ref[...])
    q_ref[...] = (qkv[:, 0:D_NA] * (HEAD_DIM ** -0.5)).astype(BF16)
    k_ref[...] = qkv[:, D_NA:2 * D_NA].astype(BF16)
    v_ref[...] = qkv[:, 2 * D_NA:3 * D_NA].astype(BF16)
    ps_ref[...] = _dot(h, wsh_ref[...])
    gt_ref[...] = _sigmoid(_dot(h, wgt_ref[...]) + bg_ref[...])


def _inproj(x, g, wqkv, wsh, wgt, bg):
    t = x.shape[0]
    tm = TOK_TILE
    row = lambda w: pl.BlockSpec((tm, w), lambda i: (i, 0))
    return pl.pallas_call(
        _inproj_body,
        grid=(t // tm,),
        in_specs=[row(D_MODEL), _const_spec(g.shape), _const_spec(wqkv.shape),
                  _const_spec(wsh.shape), _const_spec(wgt.shape), _const_spec(bg.shape)],
        out_specs=[row(D_NA), row(D_NA), row(D_NA), row(D_SHIFT), row(2 * D_MODEL)],
        out_shape=[jax.ShapeDtypeStruct((t, D_NA), BF16)] * 3
        + [jax.ShapeDtypeStruct((t, D_SHIFT), F32), jax.ShapeDtypeStruct((t, 2 * D_MODEL), F32)],
        compiler_params=_params(("parallel",)),
        name="inproj",
    )(x, g, wqkv, wsh, wgt, bg)


def _na_bias_table(rpb):
    c = jnp.arange(GRID_W)
    col_start = jnp.clip(c - WIN_W // 2, 0, GRID_W - WIN_W)
    col_ok = (c[None, :] >= col_start[:, None]) & (c[None, :] < col_start[:, None] + WIN_W)
    col_idx = jnp.clip(c[None, :] - c[:, None] + (WIN_W - 1), 0, 2 * WIN_W - 2)
    onehot = (col_idx[:, :, None] == jnp.arange(2 * WIN_W - 1)[None, None, :]).astype(F32)
    cols = jnp.einsum("hdc,qkc->hqdk", rpb.astype(F32), onehot,
                      precision=lax.Precision.HIGHEST)
    cols = jnp.where(col_ok[None, :, None, :], cols, NEG_BIG)
    tab = jnp.stack([cols[:, :, WIN_H - 1 - e:2 * WIN_H - 1 - e, :] for e in range(WIN_H)])
    return tab.reshape(WIN_H, NA_HEADS, GRID_W, WIN_H * GRID_W)


def _natten_body(q_ref, kp_ref, kc_ref, kn_ref, vp_ref, vc_ref, vn_ref, bias_ref, o_ref,
                 kwin, vwin, *, rows):
    j = pl.program_id(1)
    kwin[0:TOK_TILE, :] = kp_ref[...]
    kwin[TOK_TILE:2 * TOK_TILE, :] = kc_ref[...]
    kwin[2 * TOK_TILE:3 * TOK_TILE, :] = kn_ref[...]
    vwin[0:TOK_TILE, :] = vp_ref[...]
    vwin[TOK_TILE:2 * TOK_TILE, :] = vc_ref[...]
    vwin[2 * TOK_TILE:3 * TOK_TILE, :] = vn_ref[...]
    first_half = lax.broadcasted_iota(jnp.int32, (GRID_W, PAIR), 1) < HEAD_DIM
    n_keys = WIN_H * GRID_W
    zero_q = jnp.zeros((GRID_W, PAIR), BF16)

    def rows_body(it, carry):
        work = []
        for rr in range(NA_ROWS_PER_ITER):
            jr = it * NA_ROWS_PER_ITER + rr
            r = j * ROWS_PER_TILE + jr
            rs = jnp.clip(r - WIN_H // 2, 0, rows - WIN_H)
            off = pl.multiple_of((rs - (j - 1) * ROWS_PER_TILE) * GRID_W, GRID_W)
            q0 = pl.multiple_of(jr * GRID_W, GRID_W)
            for p in range(N_PAIR):
                work.append((r - rs, off, q0, p, slice(p * PAIR, (p + 1) * PAIR)))
        scores = []
        for e, off, q0, p, ls in work:
            qp = q_ref[pl.ds(q0, GRID_W), ls]
            q2 = jnp.concatenate([jnp.where(first_half, qp, zero_q),
                                  jnp.where(first_half, zero_q, qp)], axis=0)
            scores.append(lax.dot_general(q2, kwin[pl.ds(off, n_keys), ls], _NT,
                                          preferred_element_type=F32))
        probs, sums = [], []
        for (e, off, q0, p, ls), s2 in zip(work, scores):
            pr2 = []
            for hh in range(2):
                s = s2[hh * GRID_W:(hh + 1) * GRID_W] + bias_ref[e, 2 * p + hh]
                pr = jnp.exp(s - jnp.max(s, axis=-1, keepdims=True))
                sums.append(jnp.sum(pr, axis=-1, keepdims=True))
                pr2.append(pr.astype(BF16))
            probs.append(jnp.concatenate(pr2, axis=0))
        outs = [_dot(pr2, vwin[pl.ds(off, n_keys), ls]) for (e, off, q0, p, ls), pr2 in zip(work, probs)]
        for i, ((e, off, q0, p, ls), o2) in enumerate(zip(work, outs)):
            o0 = o2[0:GRID_W] / sums[2 * i]
            o1 = o2[GRID_W:] / sums[2 * i + 1]
            o_ref[pl.ds(q0, GRID_W), ls] = jnp.where(first_half, o0, o1).astype(BF16)
        return carry

    lax.fori_loop(0, ROWS_PER_TILE // NA_ROWS_PER_ITER, rows_body, 0)


def _natten(q, k, v, bias, b, seq):
    t = q.shape[0]
    rows = seq // GRID_W
    nblk = seq // TOK_TILE
    cur = lambda bi, j: (bi * nblk + j, 0)
    prev = lambda bi, j: (bi * nblk + jnp.maximum(j - 1, 0), 0)
    nxt = lambda bi, j: (bi * nblk + jnp.minimum(j + 1, nblk - 1), 0)
    blk = lambda im: pl.BlockSpec((TOK_TILE, D_NA), im)
    return pl.pallas_call(
        functools.partial(_natten_body, rows=rows),
        grid=(b, nblk),
        in_specs=[blk(cur), blk(prev), blk(cur), blk(nxt), blk(prev), blk(cur), blk(nxt),
                  _const_spec(bias.shape)],
        out_specs=blk(cur),
        out_shape=jax.ShapeDtypeStruct((t, D_NA), BF16),
        scratch_shapes=[pltpu.VMEM((3 * TOK_TILE, D_NA), BF16),
                        pltpu.VMEM((3 * TOK_TILE, D_NA), BF16)],
        compiler_params=_params(("parallel", "parallel")),
        name="natten",
    )(q, k, k, k, v, v, v, bias)


def _split3_dot(tri, x):
    x0 = x.astype(BF16)
    r1 = x - x0.astype(F32)
    x1 = r1.astype(BF16)
    x2 = (r1 - x1.astype(F32)).astype(BF16)
    return _dot(tri, x0) + _dot(tri, x1) + _dot(tri, x2)


def _prep_body(ps_ref, pp_ref, pn_ref, mup_ref, mun_ref, w0_ref, w2_ref, a0_ref, a2_ref, g2_ref,
               kkw_ref, ka_ref, rk_ref, seg_ref, tri_ref,
               rt_ref, kt_ref, bt_ref, at_ref, be_ref, ke_ref, v_ref, pend_ref, g_ref, bonus_ref,
               *, tiles_per_seq):
    i = pl.program_id(0)
    tb = TOK_TILE
    pos = lax.rem(i, tiles_per_seq)
    p = ps_ref[...]
    prev_row = jnp.where(pos == 0, 0.0, pp_ref[7:8, :])
    next_row = jnp.where(pos == tiles_per_seq - 1, 0.0, pn_ref[0:1, :])
    rowi = lax.broadcasted_iota(jnp.int32, (tb, 1), 0)
    prev = jnp.where(rowi == 0, prev_row, pltpu.roll(p, 1, 0))
    nxt = jnp.where(rowi == tb - 1, next_row, pltpu.roll(p, tb - 1, 0))
    sh = p + mup_ref[...] * (prev - p) + mun_ref[...] * (nxt - p)

    r = sh[:, 0:D_RWKV]
    k = sh[:, D_RWKV:2 * D_RWKV]
    v = sh[:, 2 * D_RWKV:3 * D_RWKV]
    off = 3 * D_RWKV
    wl = sh[:, off:off + 2 * DECAY_LORA]
    al = sh[:, off + 2 * DECAY_LORA:off + 2 * DECAY_LORA + 2 * ICLR_LORA]
    gl = sh[:, off + 2 * DECAY_LORA + 2 * ICLR_LORA:]

    zw = w0_ref[...] + _dot(jnp.tanh(wl).astype(BF16), w2_ref[...])
    za = a0_ref[...] + _dot(al.astype(BF16), a2_ref[...])
    g_ref[...] = _dot(_sigmoid(gl).astype(BF16), g2_ref[...])
    lw = (-math.exp(-0.5)) * _sigmoid(zw)
    iclr = _sigmoid(za)

    kkv = k * kkw_ref[...]
    kkn = kkv * lax.rsqrt(_dot((kkv * kkv).astype(BF16), seg_ref[...]) + KK_EPS)
    v_ref[...] = v.astype(BF16)

    bonus_acc = jnp.zeros_like(r)
    for d in range(N_DIR):
        ds_ = slice(d * D_RWKV, (d + 1) * D_RWKV)
        icl = iclr[:, ds_]
        kd = k * (1.0 + (icl - 1.0) * ka_ref[...])
        bd = kkn * icl
        bonus_acc = bonus_acc + r * kd * rk_ref[...]
        lwd = lw[:, ds_]
        tri = tri_ref[d]
        for c in range(CHUNKS_PER_TILE):
            cs = slice(c * CHUNK, (c + 1) * CHUNK)
            x = lwd[cs]
            lp = _split3_dot(tri, x)
            lend = lp[CHUNK - 1:CHUNK] if d == 0 else lp[0:1]
            pw = jnp.exp(lp)
            pinv = jnp.exp(-lp)
            pex = jnp.exp(lp - x)
            pe = jnp.exp(lend - lp)
            rt_ref[d, cs, :] = (r[cs] * pw).astype(BF16)
            kt_ref[d, cs, :] = (kd[cs] * pinv).astype(BF16)
            bt_ref[d, cs, :] = (bd[cs] * pinv).astype(BF16)
            at_ref[d, cs, :] = (-kkn[cs] * pex).astype(BF16)
            be_ref[d, cs, :] = (bd[cs] * pe).astype(BF16)
            ke_ref[d, cs, :] = (kd[cs] * pe).astype(BF16)
            pend_ref[d, c] = jnp.exp(lend)
    bonus_ref[...] = _dot(bonus_acc.astype(BF16), seg_ref[...]) * v


def _prep(ps, seq, mup, mun, w0, w2bd, a0, a2bd, g2, kkw, ka, rk, seg, tri):
    t = ps.shape[0]
    tb = TOK_TILE
    n8 = tb // 8
    last8 = t // 8 - 1
    row = lambda w: pl.BlockSpec((tb, w), lambda i: (i, 0))
    drow = pl.BlockSpec((N_DIR, tb, D_RWKV), lambda i: (0, i, 0))
    consts = [mup, mun, w0, w2bd, a0, a2bd, g2, kkw, ka, rk, seg, tri]
    bf = jax.ShapeDtypeStruct((N_DIR, t, D_RWKV), BF16)
    return pl.pallas_call(
        functools.partial(_prep_body, tiles_per_seq=seq // tb),
        grid=(t // tb,),
        in_specs=[row(D_SHIFT),
                  pl.BlockSpec((8, D_SHIFT), lambda i: (jnp.maximum(i * n8 - 1, 0), 0)),
                  pl.BlockSpec((8, D_SHIFT), lambda i: (jnp.minimum((i + 1) * n8, last8), 0))]
        + [_const_spec(c.shape) for c in consts],
        out_specs=[drow] * 6 + [row(D_RWKV),
                                pl.BlockSpec((N_DIR, CHUNKS_PER_TILE, 1, D_RWKV), lambda i: (0, i, 0, 0)),
                                row(D_RWKV), row(D_RWKV)],
        out_shape=[bf] * 6 + [jax.ShapeDtypeStruct((t, D_RWKV), BF16),
                              jax.ShapeDtypeStruct((N_DIR, t // CHUNK, 1, D_RWKV), F32),
                              jax.ShapeDtypeStruct((t, D_RWKV), F32),
                              jax.ShapeDtypeStruct((t, D_RWKV), F32)],
        compiler_params=_params(("parallel",)),
        name="rwkv_prep",
    )(ps, ps, ps, *consts)


def _scan_body(rt_ref, kt_ref, bt_ref, at_ref, be_ref, ke_ref, v_ref, pend_ref, y_ref,
               h_scr, q_scr, m_scr, d_scr):
    d = pl.program_id(1)
    j = pl.program_id(2)

    @pl.when(j == 0)
    def _():
        h_scr[...] = jnp.zeros_like(h_scr)

    rowi = lax.broadcasted_iota(jnp.int32, (CHUNK, PAIR), 0)
    lane = lax.broadcasted_iota(jnp.int32, (CHUNK, PAIR), 1)
    diff = (rowi - (lane & (HEAD_DIM - 1))) * (1 - 2 * d)
    strict = diff > 0
    incl = diff >= 0
    first_half = lane < HEAD_DIM
    eye_hi = (lane - HEAD_DIM == rowi).astype(F32)
    first_half_sq = lax.broadcasted_iota(jnp.int32, (PAIR, PAIR), 1) < HEAD_DIM
    eye_pair = (lax.broadcasted_iota(jnp.int32, (PAIR, PAIR), 0)
                == lax.broadcasted_iota(jnp.int32, (PAIR, PAIR), 1))
    zero_bf = jnp.zeros((CHUNK, PAIR), BF16)
    n_sq = int(math.log2(CHUNK))

    def group_body(gi, carry):
        pairs = []
        for cc in range(SCAN_GROUP):
            c = gi * SCAN_GROUP + cc
            t0 = pl.multiple_of(c * CHUNK, CHUNK)
            for p in range(N_PAIR):
                ls = slice(p * PAIR, (p + 1) * PAIR)
                ld = lambda ref: ref[0, pl.ds(t0, CHUNK), ls]
                pairs.append(dict(c=c, t0=t0, p=p, ls=ls, rt=ld(rt_ref), kt=ld(kt_ref), bt=ld(bt_ref),
                                  at=ld(at_ref), be=ld(be_ref), ke=ld(ke_ref),
                                  v=v_ref[pl.ds(t0, CHUNK), ls], pend=pend_ref[0, c, :, ls]))
        sel = lambda a, hh: jnp.where(first_half, a, zero_bf) if hh == 0 else jnp.where(first_half, zero_bf, a)
        for pr in pairs:
            lhs = jnp.concatenate([sel(pr["at"], 0), sel(pr["rt"], 0), sel(pr["at"], 1), sel(pr["rt"], 1)],
                                  axis=0)
            pr["aa"] = lax.dot_general(lhs, jnp.concatenate([pr["bt"], pr["kt"]], axis=0), _NT,
                                       preferred_element_type=F32)
        for pr in pairs:
            pr["aat"], pr["aab"], pr["x"] = [], [], []
            for hh in range(2):
                blk = pr["aa"][hh * 2 * CHUNK:(hh + 1) * 2 * CHUNK]
                aat = jnp.where(strict, blk[0:CHUNK], 0.0)
                pr["aat"].append(aat.astype(BF16))
                pr["aab"].append(jnp.where(incl, blk[CHUNK:], 0.0).astype(BF16))
                pr["x"].append(jnp.where(first_half, aat, eye_hi))
        for pr in pairs:
            zv = jnp.concatenate([zero_bf, pr["v"]], axis=0)
            pr["x1"] = _dot(jnp.concatenate(pr["aat"], axis=0), zv).astype(BF16)
        for k in range(n_sq):
            res = []
            for pr in pairs:
                for hh in range(2):
                    xb = pr["x"][hh].astype(BF16)
                    res.append(_dot(xb[:, 0:CHUNK], xb))
            it = iter(res)
            for pr in pairs:
                for hh in range(2):
                    r = next(it)
                    pr["x"][hh] = jnp.where(first_half, r, pr["x"][hh] + r)
        for pr in pairs:
            pr["r2"] = []
            for hh in range(2):
                w = jnp.concatenate([pr["at"], pr["x1"][hh * CHUNK:(hh + 1) * CHUNK]], axis=1)
                tinv = pltpu.roll(pr["x"][hh], HEAD_DIM, 1)[:, 0:CHUNK].astype(BF16)
                pr["r2"].append(_dot(tinv, w))
        for pr in pairs:
            zv_wide = jnp.concatenate([zero_bf, pr["v"]], axis=1)
            pr["r2"] = [jnp.concatenate([z.astype(BF16), zv_wide], axis=0) for z in pr["r2"]]
        for pr in pairs:
            pr["o"], pr["md"] = [], []
            for hh in range(2):
                bek_t = jnp.concatenate([sel(pr["be"], hh), sel(pr["ke"], hh)], axis=0).T
                res = _dot(jnp.concatenate([pr["aab"][hh], bek_t], axis=0), pr["r2"][hh])
                pr["o"].append(res[0:CHUNK])
                pr["md"].append(res[CHUNK:])
        for pr in pairs:
            o, md, t0, ls = pr["o"], pr["md"], pr["t0"], pr["ls"]
            qh = pr["rt"].astype(F32) + jnp.where(first_half, o[0][:, :PAIR], o[1][:, :PAIR])
            q_scr[pl.ds(t0, CHUNK), ls] = qh.astype(BF16)
            y_ref[0, pl.ds(t0, CHUNK), ls] = jnp.where(first_half, o[0][:, PAIR:], o[1][:, PAIR:])
            mbd = (jnp.where(eye_pair, pr["pend"], 0.0)
                   + jnp.where(first_half_sq, md[0][:, :PAIR], md[1][:, :PAIR]))
            m_scr[pr["c"], pr["p"]] = mbd.astype(BF16)
            d_scr[pr["c"], pr["p"]] = jnp.where(first_half_sq, md[0][:, PAIR:], md[1][:, PAIR:])
        return carry

    lax.fori_loop(0, CHUNKS_PER_TILE // SCAN_GROUP, group_body, 0)

    def chunk_body(ci, carry):
        c = jnp.where(d == 0, ci, CHUNKS_PER_TILE - 1 - ci)
        t0 = pl.multiple_of(c * CHUNK, CHUNK)
        hbs = [h_scr[p].astype(BF16) for p in range(N_PAIR)]
        ys = [_dot(q_scr[pl.ds(t0, CHUNK), p * PAIR:(p + 1) * PAIR], hbs[p]) for p in range(N_PAIR)]
        hs = [_dot(m_scr[c, p], hbs[p]) for p in range(N_PAIR)]
        for p in range(N_PAIR):
            ls = slice(p * PAIR, (p + 1) * PAIR)
            y_ref[0, pl.ds(t0, CHUNK), ls] = y_ref[0, pl.ds(t0, CHUNK), ls] + ys[p]
            h_scr[p] = hs[p] + d_scr[c, p]
        return carry

    lax.fori_loop(0, CHUNKS_PER_TILE, chunk_body, 0)


def _scan(rt, kt, bt, at, be, ke, v, pend, b, seq):
    t = v.shape[0]
    tc = TOK_TILE
    nblk = seq // tc
    tok = lambda bi, d, j: bi * nblk + jnp.where(d == 0, j, nblk - 1 - j)
    dblk = pl.BlockSpec((1, tc, D_RWKV), lambda bi, d, j: (d, tok(bi, d, j), 0))
    return pl.pallas_call(
        _scan_body,
        grid=(b, N_DIR, nblk),
        in_specs=[dblk] * 6
        + [pl.BlockSpec((tc, D_RWKV), lambda bi, d, j: (tok(bi, d, j), 0)),
           pl.BlockSpec((1, CHUNKS_PER_TILE, 1, D_RWKV), lambda bi, d, j: (d, tok(bi, d, j), 0, 0))],
        out_specs=dblk,
        out_shape=jax.ShapeDtypeStruct((N_DIR, t, D_RWKV), F32),
        scratch_shapes=[pltpu.VMEM((N_PAIR, PAIR, PAIR), F32),
                        pltpu.VMEM((tc, D_RWKV), BF16),
                        pltpu.VMEM((CHUNKS_PER_TILE, N_PAIR, PAIR, PAIR), BF16),
                        pltpu.VMEM((CHUNKS_PER_TILE, N_PAIR, PAIR, PAIR), F32)],
        compiler_params=_params(("parallel", "parallel", "arbitrary")),
        name="rwkv_scan",
    )(rt, kt, bt, at, be, ke, v, pend)


def _merge_body(y_ref, bonus_ref, g_ref, ona_ref, gt_ref, x_ref, seg_ref, lnw_ref, lnb_ref,
                wpa_ref, wpb_ref, wout_ref, npost_ref, o_ref):
    inv_n = 1.0 / HEAD_DIM
    y = y_ref[0] + y_ref[1]
    mean = _dot(y.astype(BF16), seg_ref[...]) * inv_n
    yc = y - mean
    var = _dot((yc * yc).astype(BF16), seg_ref[...]) * inv_n
    yn = yc * lax.rsqrt(var + GN_EPS) * lnw_ref[...] + lnb_ref[...]
    orw = ((yn + bonus_ref[...]) * g_ref[...]).astype(BF16)
    gates = gt_ref[...]
    merged = (gates[:, :D_MODEL] * _dot(ona_ref[...], wpa_ref[...])
              + gates[:, D_MODEL:] * _dot(orw, wpb_ref[...]))
    z = _dot(merged.astype(BF16), wout_ref[...])
    ms = jnp.mean(z * z, axis=-1, keepdims=True)
    o_ref[...] = x_ref[...] + z * lax.rsqrt(ms + RMS_EPS) * npost_ref[...]


def _merge(y, bonus, g, ona, gates, x, seg, lnw, lnb, wpa, wpb, wout, npost):
    t = x.shape[0]
    tm = TOK_TILE
    row = lambda w: pl.BlockSpec((tm, w), lambda i: (i, 0))
    consts = [seg, lnw, lnb, wpa, wpb, wout, npost]
    return pl.pallas_call(
        _merge_body,
        grid=(t // tm,),
        in_specs=[pl.BlockSpec((N_DIR, tm, D_RWKV), lambda i: (0, i, 0)),
                  row(D_RWKV), row(D_RWKV), row(D_NA), row(2 * D_MODEL), row(D_MODEL)]
        + [_const_spec(c.shape) for c in consts],
        out_specs=row(D_MODEL),
        out_shape=jax.ShapeDtypeStruct((t, D_MODEL), F32),
        compiler_params=_params(("parallel",)),
        name="merge",
    )(y, bonus, g, ona, gates, x, *consts)


def _ffn_body(x_ref, npre_ref, wup_ref, wdn_ref, npost_ref, o_ref):
    x = x_ref[...]
    ms = jnp.mean(x * x, axis=-1, keepdims=True)
    h = (x * lax.rsqrt(ms + RMS_EPS) * npre_ref[...]).astype(BF16)
    u = jnp.maximum(_dot(h, wup_ref[...]), 0.0)
    f = _dot((u * u).astype(BF16), wdn_ref[...])
    ms = jnp.mean(f * f, axis=-1, keepdims=True)
    o_ref[...] = x + f * lax.rsqrt(ms + RMS_EPS) * npost_ref[...]


def _ffn(x, npre, wup, wdn, npost):
    t = x.shape[0]
    tm = TOK_TILE
    row = pl.BlockSpec((tm, D_MODEL), lambda i: (i, 0))
    consts = [npre, wup, wdn, npost]
    return pl.pallas_call(
        _ffn_body,
        grid=(t // tm,),
        in_specs=[row] + [_const_spec(c.shape) for c in consts],
        out_specs=row,
        out_shape=jax.ShapeDtypeStruct((t, D_MODEL), F32),
        compiler_params=_params(("parallel",)),
        name="ffn",
    )(x, *consts)


def _block_diag2(w):
    z = jnp.zeros_like(w[0])
    return jnp.concatenate([jnp.concatenate([w[0], z], axis=1),
                            jnp.concatenate([z, w[1]], axis=1)], axis=0)


def _layer_consts(w_in, b_gate, rpb, mu_prev, mu_next, w0, w2, a0, a2, g2, k_k, k_a, r_k,
                  ln_w, ln_b, w_proj_a, w_proj_b, w_out, w_up, w_down,
                  norm_pre_mix, norm_post_mix, norm_pre_ffn, norm_post_ffn):
    rowv = lambda a: a.reshape(1, -1).astype(F32)
    head = jnp.arange(D_RWKV) // HEAD_DIM
    idx = jnp.arange(CHUNK)
    tril = (idx[:, None] >= idx[None, :])
    return dict(
        wqkv=w_in[:, :3 * D_NA].astype(BF16),
        wsh=w_in[:, 3 * D_NA:3 * D_NA + D_SHIFT].astype(BF16),
        wgt=w_in[:, 3 * D_NA + D_SHIFT:].astype(BF16),
        bg=rowv(b_gate), bias=_na_bias_table(rpb),
        mup=rowv(mu_prev), mun=rowv(mu_next),
        w0=rowv(w0), w2bd=_block_diag2(w2).astype(BF16),
        a0=rowv(a0), a2bd=_block_diag2(a2).astype(BF16),
        g2=g2.astype(BF16), kkw=rowv(k_k), ka=rowv(k_a), rk=rowv(r_k),
        seg=(head[:, None] == head[None, :]).astype(BF16),
        tri=jnp.stack([tril, tril.T]).astype(BF16),
        lnw=rowv(ln_w), lnb=rowv(ln_b),
        wpa=w_proj_a.astype(BF16), wpb=w_proj_b.astype(BF16), wout=w_out.astype(BF16),
        wup=w_up.astype(BF16), wdn=w_down.astype(BF16),
        npre_mix=rowv(norm_pre_mix), npost_mix=rowv(norm_post_mix),
        npre_ffn=rowv(norm_pre_ffn), npost_ffn=rowv(norm_post_ffn),
    )


def _encoder_layer(x, b, seq, c):
    q, k, v, ps, gates = _inproj(x, c["npre_mix"], c["wqkv"], c["wsh"], c["wgt"], c["bg"])
    ona = _natten(q, k, v, c["bias"], b, seq)
    rt, kt, bt, at, be, ke, vr, pend, g, bonus = _prep(
        ps, seq, c["mup"], c["mun"], c["w0"], c["w2bd"], c["a0"], c["a2bd"], c["g2"],
        c["kkw"], c["ka"], c["rk"], c["seg"], c["tri"])
    y = _scan(rt, kt, bt, at, be, ke, vr, pend, b, seq)
    x1 = _merge(y, bonus, g, ona, gates, x, c["seg"], c["lnw"], c["lnb"],
                c["wpa"], c["wpb"], c["wout"], c["npost_mix"])
    return _ffn(x1, c["npre_ffn"], c["wup"], c["wdn"], c["npost_ffn"])


def _trunk(x, layer_consts):
    b, seq, dm = x.shape
    assert dm == D_MODEL and seq % TOK_TILE == 0 and seq // GRID_W >= WIN_H
    h = x.reshape(b * seq, dm)
    for c in layer_consts:
        h = _encoder_layer(h, b, seq, c)
    return h.reshape(b, seq, dm)


def kernel(x_prompt, x_sample, w_in, b_gate, rpb, mu_prev, mu_next, w0, w2, a0, a2, g2, k_k, k_a, r_k, ln_w, ln_b, w_proj_a, w_proj_b, w_out, w_up, w_down, norm_pre_mix, norm_post_mix, norm_pre_ffn, norm_post_ffn):
    stacked = (w_in, b_gate, rpb, mu_prev, mu_next, w0, w2, a0, a2, g2, k_k, k_a, r_k,
               ln_w, ln_b, w_proj_a, w_proj_b, w_out, w_up, w_down,
               norm_pre_mix, norm_post_mix, norm_pre_ffn, norm_post_ffn)
    depth = w_in.shape[0]
    layer_consts = [_layer_consts(*[p[i] for p in stacked]) for i in range(depth)]
    return (_trunk(x_prompt, layer_consts), _trunk(x_sample, layer_consts))
```

```python
import functools
import math

import jax
import jax.numpy as jnp
from jax import lax
from jax.experimental import pallas as pl
from jax.experimental.pallas import tpu as pltpu

F32 = jnp.float32
BF16 = jnp.bfloat16

D_MODEL = 1024
GRID_W = 64
WIN_H = 8
WIN_W = 16
NA_HEADS = 8
HEAD_DIM = 64
D_NA = NA_HEADS * HEAD_DIM
RWKV_HEADS = 8
D_RWKV = RWKV_HEADS * HEAD_DIM
DECAY_LORA = 64
ICLR_LORA = 64
GATE_LORA = 128
N_DIR = 2
D_FF = 4 * D_MODEL
D_SHIFT = 3 * D_RWKV + N_DIR * DECAY_LORA + N_DIR * ICLR_LORA + GATE_LORA
RMS_EPS = 1e-6
GN_EPS = 64e-5
KK_EPS = 1e-12
NEG_BIG = -1e30

CHUNK = 64
PAIR = 2 * HEAD_DIM
N_PAIR = D_RWKV // PAIR
TOK_TILE = 512
ROWS_PER_TILE = TOK_TILE // GRID_W
CHUNKS_PER_TILE = TOK_TILE // CHUNK
NA_ROWS_PER_ITER = 2
SCAN_GROUP = 4
V7X_VMEM_LIMIT = 56 * 1024 * 1024

_NT = (((1,), (1,)), ((), ()))


def _dot(a, b):
    return jnp.dot(a, b, preferred_element_type=F32)


def _sigmoid(x):
    return 1.0 / (1.0 + jnp.exp(-x))


def _params(sem):
    return pltpu.CompilerParams(dimension_semantics=sem, vmem_limit_bytes=V7X_VMEM_LIMIT)


def _const_spec(shape):
    nd = len(shape)
    return pl.BlockSpec(shape, lambda *_: (0,) * nd, pipeline_mode=pl.Buffered(1))


def _inproj_body(x_ref, g_ref, wqkv_ref, wsh_ref, wgt_ref, bg_ref,
                 q_ref, k_ref, v_ref, ps_ref, gt_ref):
    x = x_ref[...]
    ms = jnp.mean(x * x, axis=-1, keepdims=True)
    h = (x * lax.rsqrt(ms + RMS_EPS) * g_ref[...]).astype(BF16)
    qkv = _dot(h, wqkv_ref[...])
    q_ref[...] = (qkv[:, 0:D_NA] * (HEAD_DIM ** -0.5)).astype(BF16)
    k_ref[...] = qkv[:, D_NA:2 * D_NA].astype(BF16)
    v_ref[...] = qkv[:, 2 * D_NA:3 * D_NA].astype(BF16)
    ps_ref[...] = _dot(h, wsh_ref[...])
    gt_ref[...] = _sigmoid(_dot(h, wgt_ref[...]) + bg_ref[...]).astype(BF16)


def _inproj(x, g, wqkv, wsh, wgt, bg):
    t = x.shape[0]
    tm = TOK_TILE
    row = lambda w: pl.BlockSpec((tm, w), lambda i: (i, 0))
    return pl.pallas_call(
        _inproj_body,
        grid=(t // tm,),
        in_specs=[row(D_MODEL), _const_spec(g.shape), _const_spec(wqkv.shape),
                  _const_spec(wsh.shape), _const_spec(wgt.shape), _const_spec(bg.shape)],
        out_specs=[row(D_NA), row(D_NA), row(D_NA), row(D_SHIFT), row(2 * D_MODEL)],
        out_shape=[jax.ShapeDtypeStruct((t, D_NA), BF16)] * 3
        + [jax.ShapeDtypeStruct((t, D_SHIFT), F32), jax.ShapeDtypeStruct((t, 2 * D_MODEL), BF16)],
        compiler_params=_params(("parallel",)),
        name="inproj",
    )(x, g, wqkv, wsh, wgt, bg)


def _na_bias_table(rpb):
    c = jnp.arange(GRID_W)
    col_start = jnp.clip(c - WIN_W // 2, 0, GRID_W - WIN_W)
    col_ok = (c[None, :] >= col_start[:, None]) & (c[None, :] < col_start[:, None] + WIN_W)
    col_idx = jnp.clip(c[None, :] - c[:, None] + (WIN_W - 1), 0, 2 * WIN_W - 2)
    onehot = (col_idx[:, :, None] == jnp.arange(2 * WIN_W - 1)[None, None, :]).astype(F32)
    cols = jnp.einsum("hdc,qkc->hqdk", rpb.astype(F32), onehot,
                      precision=lax.Precision.HIGHEST)
    cols = jnp.where(col_ok[None, :, None, :], cols, NEG_BIG)
    tab = jnp.stack([cols[:, :, WIN_H - 1 - e:2 * WIN_H - 1 - e, :] for e in range(WIN_H)])
    return tab.reshape(WIN_H, NA_HEADS, GRID_W, WIN_H * GRID_W)


def _natten_body(q_ref, kp_ref, kc_ref, kn_ref, vp_ref, vc_ref, vn_ref, bias_ref, o_ref,
                 kwin, vwin, *, rows):
    j = pl.program_id(1)
    kwin[0:TOK_TILE, :] = kp_ref[...]
    kwin[TOK_TILE:2 * TOK_TILE, :] = kc_ref[...]
    kwin[2 * TOK_TILE:3 * TOK_TILE, :] = kn_ref[...]
    vwin[0:TOK_TILE, :] = vp_ref[...]
    vwin[TOK_TILE:2 * TOK_TILE, :] = vc_ref[...]
    vwin[2 * TOK_TILE:3 * TOK_TILE, :] = vn_ref[...]
    first_half = lax.broadcasted_iota(jnp.int32, (GRID_W, PAIR), 1) < HEAD_DIM
    n_keys = WIN_H * GRID_W
    zero_q = jnp.zeros((GRID_W, PAIR), BF16)

    def rows_body(it, carry):
        work = []
        for rr in range(NA_ROWS_PER_ITER):
            jr = it * NA_ROWS_PER_ITER + rr
            r = j * ROWS_PER_TILE + jr
            rs = jnp.clip(r - WIN_H // 2, 0, rows - WIN_H)
            off = pl.multiple_of((rs - (j - 1) * ROWS_PER_TILE) * GRID_W, GRID_W)
            q0 = pl.multiple_of(jr * GRID_W, GRID_W)
            for p in range(N_PAIR):
                work.append((r - rs, off, q0, p, slice(p * PAIR, (p + 1) * PAIR)))
        scores = []
        for e, off, q0, p, ls in work:
            qp = q_ref[pl.ds(q0, GRID_W), ls]
            q2 = jnp.concatenate([jnp.where(first_half, qp, zero_q),
                                  jnp.where(first_half, zero_q, qp)], axis=0)
            scores.append(lax.dot_general(q2, kwin[pl.ds(off, n_keys), ls], _NT,
                                          preferred_element_type=F32))
        probs, sums = [], []
        for (e, off, q0, p, ls), s2 in zip(work, scores):
            pr2 = []
            for hh in range(2):
                s = s2[hh * GRID_W:(hh + 1) * GRID_W] + bias_ref[e, 2 * p + hh]
                pr = jnp.exp(s - jnp.max(s, axis=-1, keepdims=True))
                sums.append(jnp.sum(pr, axis=-1, keepdims=True))
                pr2.append(pr.astype(BF16))
            probs.append(jnp.concatenate(pr2, axis=0))
        outs = [_dot(pr2, vwin[pl.ds(off, n_keys), ls]) for (e, off, q0, p, ls), pr2 in zip(work, probs)]
        for i, ((e, off, q0, p, ls), o2) in enumerate(zip(work, outs)):
            o0 = o2[0:GRID_W] / sums[2 * i]
            o1 = o2[GRID_W:] / sums[2 * i + 1]
            o_ref[pl.ds(q0, GRID_W), ls] = jnp.where(first_half, o0, o1).astype(BF16)
        return carry

    lax.fori_loop(0, ROWS_PER_TILE // NA_ROWS_PER_ITER, rows_body, 0)


def _natten(q, k, v, bias, b, seq):
    t = q.shape[0]
    rows = seq // GRID_W
    nblk = seq // TOK_TILE
    cur = lambda bi, j: (bi * nblk + j, 0)
    prev = lambda bi, j: (bi * nblk + jnp.maximum(j - 1, 0), 0)
    nxt = lambda bi, j: (bi * nblk + jnp.minimum(j + 1, nblk - 1), 0)
    blk = lambda im: pl.BlockSpec((TOK_TILE, D_NA), im)
    return pl.pallas_call(
        functools.partial(_natten_body, rows=rows),
        grid=(b, nblk),
        in_specs=[blk(cur), blk(prev), blk(cur), blk(nxt), blk(prev), blk(cur), blk(nxt),
                  _const_spec(bias.shape)],
        out_specs=blk(cur),
        out_shape=jax.ShapeDtypeStruct((t, D_NA), BF16),
        scratch_shapes=[pltpu.VMEM((3 * TOK_TILE, D_NA), BF16),
                        pltpu.VMEM((3 * TOK_TILE, D_NA), BF16)],
        compiler_params=_params(("parallel", "parallel")),
        name="natten",
    )(q, k, k, k, v, v, v, bias)


def _split3_dot(tri, x):
    x0 = x.astype(BF16)
    r1 = x - x0.astype(F32)
    x1 = r1.astype(BF16)
    x2 = (r1 - x1.astype(F32)).astype(BF16)
    return _dot(tri, x0) + _dot(tri, x1) + _dot(tri, x2)


def _prep_body(ps_ref, pp_ref, pn_ref, mup_ref, mun_ref, w0_ref, w2_ref, a0_ref, a2_ref, g2_ref,
               kkw_ref, ka_ref, rk_ref, seg_ref, tri_ref,
               rt_ref, kt_ref, bt_ref, at_ref, be_ref, ke_ref, v_ref, pend_ref, g_ref, bonus_ref,
               *, tiles_per_seq):
    i = pl.program_id(0)
    tb = TOK_TILE
    pos = lax.rem(i, tiles_per_seq)
    p = ps_ref[...]
    prev_row = jnp.where(pos == 0, 0.0, pp_ref[7:8, :])
    next_row = jnp.where(pos == tiles_per_seq - 1, 0.0, pn_ref[0:1, :])
    rowi = lax.broadcasted_iota(jnp.int32, (tb, 1), 0)
    prev = jnp.where(rowi == 0, prev_row, pltpu.roll(p, 1, 0))
    nxt = jnp.where(rowi == tb - 1, next_row, pltpu.roll(p, tb - 1, 0))
    sh = p + mup_ref[...] * (prev - p) + mun_ref[...] * (nxt - p)

    r = sh[:, 0:D_RWKV]
    k = sh[:, D_RWKV:2 * D_RWKV]
    v = sh[:, 2 * D_RWKV:3 * D_RWKV]
    off = 3 * D_RWKV
    wl = sh[:, off:off + 2 * DECAY_LORA]
    al = sh[:, off + 2 * DECAY_LORA:off + 2 * DECAY_LORA + 2 * ICLR_LORA]
    gl = sh[:, off + 2 * DECAY_LORA + 2 * ICLR_LORA:]

    zw = w0_ref[...] + _dot(jnp.tanh(wl).astype(BF16), w2_ref[...])
    za = a0_ref[...] + _dot(al.astype(BF16), a2_ref[...])
    g_ref[...] = _dot(_sigmoid(gl).astype(BF16), g2_ref[...]).astype(BF16)
    lw = (-math.exp(-0.5)) * _sigmoid(zw)
    iclr = _sigmoid(za)

    kkv = k * kkw_ref[...]
    kkn = kkv * lax.rsqrt(_dot((kkv * kkv).astype(BF16), seg_ref[...]) + KK_EPS)
    v_ref[...] = v.astype(BF16)

    bonus_acc = jnp.zeros_like(r)
    for d in range(N_DIR):
        ds_ = slice(d * D_RWKV, (d + 1) * D_RWKV)
        icl = iclr[:, ds_]
        kd = k * (1.0 + (icl - 1.0) * ka_ref[...])
        bd = kkn * icl
        bonus_acc = bonus_acc + r * kd * rk_ref[...]
        lwd = lw[:, ds_]
        tri = tri_ref[d]
        for c in range(CHUNKS_PER_TILE):
            cs = slice(c * CHUNK, (c + 1) * CHUNK)
            x = lwd[cs]
            lp = _split3_dot(tri, x)
            lend = lp[CHUNK - 1:CHUNK] if d == 0 else lp[0:1]
            pw = jnp.exp(lp)
            pinv = jnp.exp(-lp)
            pex = jnp.exp(lp - x)
            pe = jnp.exp(lend - lp)
            rt_ref[d, cs, :] = (r[cs] * pw).astype(BF16)
            kt_ref[d, cs, :] = (kd[cs] * pinv).astype(BF16)
            bt_ref[d, cs, :] = (bd[cs] * pinv).astype(BF16)
            at_ref[d, cs, :] = (-kkn[cs] * pex).astype(BF16)
            be_ref[d, cs, :] = (bd[cs] * pe).astype(BF16)
            ke_ref[d, cs, :] = (kd[cs] * pe).astype(BF16)
            pend_ref[d, c] = jnp.exp(lend)
    bonus_ref[...] = (_dot(bonus_acc.astype(BF16), seg_ref[...]) * v).astype(BF16)


def _prep(ps, seq, mup, mun, w0, w2bd, a0, a2bd, g2, kkw, ka, rk, seg, tri):
    t = ps.shape[0]
    tb = TOK_TILE
    n8 = tb // 8
    last8 = t // 8 - 1
    row = lambda w: pl.BlockSpec((tb, w), lambda i: (i, 0))
    drow = pl.BlockSpec((N_DIR, tb, D_RWKV), lambda i: (0, i, 0))
    consts = [mup, mun, w0, w2bd, a0, a2bd, g2, kkw, ka, rk, seg, tri]
    bf = jax.ShapeDtypeStruct((N_DIR, t, D_RWKV), BF16)
    return pl.pallas_call(
        functools.partial(_prep_body, tiles_per_seq=seq // tb),
        grid=(t // tb,),
        in_specs=[row(D_SHIFT),
                  pl.BlockSpec((8, D_SHIFT), lambda i: (jnp.maximum(i * n8 - 1, 0), 0)),
                  pl.BlockSpec((8, D_SHIFT), lambda i: (jnp.minimum((i + 1) * n8, last8), 0))]
        + [_const_spec(c.shape) for c in consts],
        out_specs=[drow] * 6 + [row(D_RWKV),
                                pl.BlockSpec((N_DIR, CHUNKS_PER_TILE, 1, D_RWKV), lambda i: (0, i, 0, 0)),
                                row(D_RWKV), row(D_RWKV)],
        out_shape=[bf] * 6 + [jax.ShapeDtypeStruct((t, D_RWKV), BF16),
                              jax.ShapeDtypeStruct((N_DIR, t // CHUNK, 1, D_RWKV), F32),
                              jax.ShapeDtypeStruct((t, D_RWKV), BF16),
                              jax.ShapeDtypeStruct((t, D_RWKV), BF16)],
        compiler_params=_params(("parallel",)),
        name="rwkv_prep",
    )(ps, ps, ps, *consts)


def _scan_factors(d, rt_ref, kt_ref, bt_ref, at_ref, be_ref, ke_ref, v_ref, pend_ref,
                  y_ref, q_scr, m_scr, d_scr):
    sign = 1 - 2 * d
    rowi2 = lax.broadcasted_iota(jnp.int32, (CHUNK, 2 * PAIR), 0)
    lane2 = lax.broadcasted_iota(jnp.int32, (CHUNK, 2 * PAIR), 1)
    diff = (rowi2 - (lane2 & (HEAD_DIM - 1))) * sign
    strict2 = diff > 0
    incl2 = diff >= 0
    low2 = (lane2 & HEAD_DIM) == 0
    rowi = lax.broadcasted_iota(jnp.int32, (CHUNK, PAIR), 0)
    lane = lax.broadcasted_iota(jnp.int32, (CHUNK, PAIR), 1)
    first_half = lane < HEAD_DIM
    eye_lo = (lane == rowi).astype(F32)
    eye_hi = (lane - HEAD_DIM == rowi).astype(F32)
    rowp = lax.broadcasted_iota(jnp.int32, (PAIR, PAIR), 0)
    lanep = lax.broadcasted_iota(jnp.int32, (PAIR, PAIR), 1)
    same_half = (rowp < HEAD_DIM) == (lanep < HEAD_DIM)
    eye_pair = rowp == lanep
    zero = jnp.zeros((CHUNK, PAIR), BF16)
    zero2 = jnp.zeros((CHUNK, 2 * PAIR), BF16)
    cat0 = lambda xs: jnp.concatenate(xs, axis=0)
    cat1 = lambda xs: jnp.concatenate(xs, axis=1)
    sel = lambda a, hh: jnp.where(first_half, a, zero) if hh == 0 else jnp.where(first_half, zero, a)
    lo2 = lambda a: jnp.where(low2, a, zero2)
    hi2 = lambda a: jnp.where(low2, zero2, a)

    def group_body(gi, carry):
        prs = []
        for cc in range(SCAN_GROUP):
            c = gi * SCAN_GROUP + cc
            t0 = pl.multiple_of(c * CHUNK, CHUNK)
            for p in range(N_PAIR):
                ls = slice(p * PAIR, (p + 1) * PAIR)
                ld = lambda ref: ref[0, pl.ds(t0, CHUNK), ls]
                prs.append(dict(c=c, t0=t0, p=p, ls=ls, rt=ld(rt_ref), kt=ld(kt_ref), bt=ld(bt_ref),
                                at=ld(at_ref), be=ld(be_ref), ke=ld(ke_ref),
                                v=v_ref[pl.ds(t0, CHUNK), ls], pend=pend_ref[0, c, :, ls]))
        for pr in prs:
            rhs = cat0([sel(pr["bt"], 0), sel(pr["kt"], 0), sel(pr["kt"], 1), sel(pr["bt"], 1)])
            aa = lax.dot_general(cat0([pr["at"], pr["rt"]]), rhs, _NT, preferred_element_type=F32)
            top = jnp.where(strict2, aa[0:CHUNK], 0.0)
            pr["top"] = top.astype(BF16)
            pr["bot"] = jnp.where(incl2, aa[CHUNK:], 0.0).astype(BF16)
            pr["x0"] = jnp.where(first_half, top[:, :PAIR], eye_hi)
            pr["x1"] = jnp.where(first_half, eye_lo, top[:, PAIR:])
        for pr in prs:
            v = pr["v"]
            pr["akv"] = _dot(pr["top"], cat0([zero2, cat1([v, zero]), cat1([zero, v]), zero2])).astype(BF16)
        for _ in range(int(math.log2(CHUNK))):
            res = []
            for pr in prs:
                x0b, x1b = pr["x0"].astype(BF16), pr["x1"].astype(BF16)
                res.append(_dot(jnp.where(first_half, x0b, x1b),
                                cat0([cat1([x0b, zero]), cat1([zero, x1b])])))
            for pr, r in zip(prs, res):
                pr["x0"] = jnp.where(first_half, r[:, :PAIR], pr["x0"] + r[:, :PAIR])
                pr["x1"] = jnp.where(first_half, pr["x1"] + r[:, PAIR:], r[:, PAIR:])
        for pr in prs:
            w1 = cat1([sel(pr["at"], 1), sel(pr["akv"][:, PAIR:], 1)])
            w0 = cat1([sel(pr["at"], 0), sel(pr["akv"][:, :PAIR], 0)])
            z = _dot((pr["x0"] + pr["x1"]).astype(BF16), cat0([w1, w0]))
            pr["au"] = z.astype(BF16)
            pr["zv"] = cat1([zero, pr["v"]])
        for pr in prs:
            au, zv = pr["au"], pr["zv"]
            pr["o"] = _dot(pr["bot"], cat0([lo2(au), lo2(zv), hi2(zv), hi2(au)]))
            pr["md"] = _dot(cat0([pr["be"], pr["ke"]]).T, cat0([au, zv]))
        for pr in prs:
            o, md, t0, ls = pr["o"], pr["md"], pr["t0"], pr["ls"]
            q_scr[pl.ds(t0, CHUNK), ls] = (pr["rt"].astype(F32) + o[:, :PAIR]).astype(BF16)
            y_ref[pl.ds(t0, CHUNK), ls] = o[:, PAIR:]
            mbd = jnp.where(eye_pair, pr["pend"], 0.0) + jnp.where(same_half, md[:, :PAIR], 0.0)
            m_scr[pr["c"], pr["p"]] = mbd.astype(BF16)
            d_scr[pr["c"], pr["p"]] = jnp.where(same_half, md[:, PAIR:], 0.0)
        return carry

    lax.fori_loop(0, CHUNKS_PER_TILE // SCAN_GROUP, group_body, 0)


def _scan_body(*refs):
    n_in = 8
    ins = (refs[0:n_in], refs[n_in:2 * n_in])
    ys = refs[2 * n_in:2 * n_in + 2]
    h_scr, q_scr, m_scr, d_scr = refs[2 * n_in + 2:]

    @pl.when(pl.program_id(1) == 0)
    def _():
        h_scr[...] = jnp.zeros_like(h_scr)

    for d in range(N_DIR):
        _scan_factors(d, *ins[d], ys[d], q_scr.at[d], m_scr.at[d], d_scr.at[d])

    def chunk_body(ci, carry):
        work = []
        for d in range(N_DIR):
            c = ci if d == 0 else CHUNKS_PER_TILE - 1 - ci
            t0 = pl.multiple_of(c * CHUNK, CHUNK)
            for p in range(N_PAIR):
                work.append((d, c, t0, p, slice(p * PAIR, (p + 1) * PAIR)))
        hbs = [h_scr[d, p].astype(BF16) for d, c, t0, p, ls in work]
        yd = [_dot(q_scr[d, pl.ds(t0, CHUNK), ls], hb) for (d, c, t0, p, ls), hb in zip(work, hbs)]
        hd = [_dot(m_scr[d, c, p], hb) for (d, c, t0, p, ls), hb in zip(work, hbs)]
        for (d, c, t0, p, ls), y, h in zip(work, yd, hd):
            ys[d][pl.ds(t0, CHUNK), ls] = ys[d][pl.ds(t0, CHUNK), ls] + y
            h_scr[d, p] = h + d_scr[d, c, p]
        return carry

    lax.fori_loop(0, CHUNKS_PER_TILE, chunk_body, 0)


def _scan(rt, kt, bt, at, be, ke, v, pend, b, seq):
    t = v.shape[0]
    tc = TOK_TILE
    nblk = seq // tc
    tok = (lambda bi, j: bi * nblk + j, lambda bi, j: bi * nblk + nblk - 1 - j)
    in_specs, args = [], []
    for d in range(N_DIR):
        dblk = pl.BlockSpec((1, tc, D_RWKV), lambda bi, j, d=d: (d, tok[d](bi, j), 0))
        in_specs += [dblk] * 6 + [
            pl.BlockSpec((tc, D_RWKV), lambda bi, j, d=d: (tok[d](bi, j), 0)),
            pl.BlockSpec((1, CHUNKS_PER_TILE, 1, D_RWKV), lambda bi, j, d=d: (d, tok[d](bi, j), 0, 0))]
        args += [rt, kt, bt, at, be, ke, v, pend]
    return pl.pallas_call(
        _scan_body,
        grid=(b, nblk),
        in_specs=in_specs,
        out_specs=[pl.BlockSpec((tc, D_RWKV), lambda bi, j, d=d: (tok[d](bi, j), 0)) for d in range(N_DIR)],
        out_shape=[jax.ShapeDtypeStruct((t, D_RWKV), F32)] * N_DIR,
        scratch_shapes=[pltpu.VMEM((N_DIR, N_PAIR, PAIR, PAIR), F32),
                        pltpu.VMEM((N_DIR, tc, D_RWKV), BF16),
                        pltpu.VMEM((N_DIR, CHUNKS_PER_TILE, N_PAIR, PAIR, PAIR), BF16),
                        pltpu.VMEM((N_DIR, CHUNKS_PER_TILE, N_PAIR, PAIR, PAIR), F32)],
        compiler_params=_params(("parallel", "arbitrary")),
        name="rwkv_scan",
    )(*args)


def _merge_body(y0_ref, y1_ref, bonus_ref, g_ref, ona_ref, gt_ref, x_ref, seg_ref, lnw_ref, lnb_ref,
                wpa_ref, wpb_ref, wout_ref, npost_ref, o_ref):
    inv_n = 1.0 / HEAD_DIM
    y = y0_ref[...] + y1_ref[...]
    mean = _dot(y.astype(BF16), seg_ref[...]) * inv_n
    yc = y - mean
    var = _dot((yc * yc).astype(BF16), seg_ref[...]) * inv_n
    yn = yc * lax.rsqrt(var + GN_EPS) * lnw_ref[...] + lnb_ref[...]
    orw = ((yn + bonus_ref[...]) * g_ref[...]).astype(BF16)
    gates = gt_ref[...]
    merged = (gates[:, :D_MODEL] * _dot(ona_ref[...], wpa_ref[...])
              + gates[:, D_MODEL:] * _dot(orw, wpb_ref[...]))
    z = _dot(merged.astype(BF16), wout_ref[...])
    ms = jnp.mean(z * z, axis=-1, keepdims=True)
    o_ref[...] = x_ref[...] + z * lax.rsqrt(ms + RMS_EPS) * npost_ref[...]


def _merge(y0, y1, bonus, g, ona, gates, x, seg, lnw, lnb, wpa, wpb, wout, npost):
    t = x.shape[0]
    tm = TOK_TILE
    row = lambda w: pl.BlockSpec((tm, w), lambda i: (i, 0))
    consts = [seg, lnw, lnb, wpa, wpb, wout, npost]
    return pl.pallas_call(
        _merge_body,
        grid=(t // tm,),
        in_specs=[row(D_RWKV), row(D_RWKV), row(D_RWKV), row(D_RWKV), row(D_NA), row(2 * D_MODEL), row(D_MODEL)]
        + [_const_spec(c.shape) for c in consts],
        out_specs=row(D_MODEL),
        out_shape=jax.ShapeDtypeStruct((t, D_MODEL), F32),
        compiler_params=_params(("parallel",)),
        name="merge",
    )(y0, y1, bonus, g, ona, gates, x, *consts)


def _ffn_body(x_ref, npre_ref, wup_ref, wdn_ref, npost_ref, o_ref):
    x = x_ref[...]
    ms = jnp.mean(x * x, axis=-1, keepdims=True)
    h = (x * lax.rsqrt(ms + RMS_EPS) * npre_ref[...]).astype(BF16)
    u = jnp.maximum(_dot(h, wup_ref[...]), 0.0)
    f = _dot((u * u).astype(BF16), wdn_ref[...])
    ms = jnp.mean(f * f, axis=-1, keepdims=True)
    o_ref[...] = x + f * lax.rsqrt(ms + RMS_EPS) * npost_ref[...]


def _ffn(x, npre, wup, wdn, npost):
    t = x.shape[0]
    tm = TOK_TILE
    row = pl.BlockSpec((tm, D_MODEL), lambda i: (i, 0))
    consts = [npre, wup, wdn, npost]
    return pl.pallas_call(
        _ffn_body,
        grid=(t // tm,),
        in_specs=[row] + [_const_spec(c.shape) for c in consts],
        out_specs=row,
        out_shape=jax.ShapeDtypeStruct((t, D_MODEL), F32),
        compiler_params=_params(("parallel",)),
        name="ffn",
    )(x, *consts)


def _block_diag2(w):
    z = jnp.zeros_like(w[0])
    return jnp.concatenate([jnp.concatenate([w[0], z], axis=1),
                            jnp.concatenate([z, w[1]], axis=1)], axis=0)


def _layer_consts(w_in, b_gate, rpb, mu_prev, mu_next, w0, w2, a0, a2, g2, k_k, k_a, r_k,
                  ln_w, ln_b, w_proj_a, w_proj_b, w_out, w_up, w_down,
                  norm_pre_mix, norm_post_mix, norm_pre_ffn, norm_post_ffn):
    rowv = lambda a: a.reshape(1, -1).astype(F32)
    head = jnp.arange(D_RWKV) // HEAD_DIM
    idx = jnp.arange(CHUNK)
    tril = (idx[:, None] >= idx[None, :])
    return dict(
        wqkv=w_in[:, :3 * D_NA].astype(BF16),
        wsh=w_in[:, 3 * D_NA:3 * D_NA + D_SHIFT].astype(BF16),
        wgt=w_in[:, 3 * D_NA + D_SHIFT:].astype(BF16),
        bg=rowv(b_gate), bias=_na_bias_table(rpb),
        mup=rowv(mu_prev), mun=rowv(mu_next),
        w0=rowv(w0), w2bd=_block_diag2(w2).astype(BF16),
        a0=rowv(a0), a2bd=_block_diag2(a2).astype(BF16),
        g2=g2.astype(BF16), kkw=rowv(k_k), ka=rowv(k_a), rk=rowv(r_k),
        seg=(head[:, None] == head[None, :]).astype(BF16),
        tri=jnp.stack([tril, tril.T]).astype(BF16),
        lnw=rowv(ln_w), lnb=rowv(ln_b),
        wpa=w_proj_a.astype(BF16), wpb=w_proj_b.astype(BF16), wout=w_out.astype(BF16),
        wup=w_up.astype(BF16), wdn=w_down.astype(BF16),
        npre_mix=rowv(norm_pre_mix), npost_mix=rowv(norm_post_mix),
        npre_ffn=rowv(norm_pre_ffn), npost_ffn=rowv(norm_post_ffn),
    )


def _encoder_layer(x, b, seq, c):
    q, k, v, ps, gates = _inproj(x, c["npre_mix"], c["wqkv"], c["wsh"], c["wgt"], c["bg"])
    ona = _natten(q, k, v, c["bias"], b, seq)
    rt, kt, bt, at, be, ke, vr, pend, g, bonus = _prep(
        ps, seq, c["mup"], c["mun"], c["w0"], c["w2bd"], c["a0"], c["a2bd"], c["g2"],
        c["kkw"], c["ka"], c["rk"], c["seg"], c["tri"])
    y0, y1 = _scan(rt, kt, bt, at, be, ke, vr, pend, b, seq)
    x1 = _merge(y0, y1, bonus, g, ona, gates, x, c["seg"], c["lnw"], c["lnb"],
                c["wpa"], c["wpb"], c["wout"], c["npost_mix"])
    return _ffn(x1, c["npre_ffn"], c["wup"], c["wdn"], c["npost_ffn"])


def _trunk(x, layer_consts):
    b, seq, dm = x.shape
    assert dm == D_MODEL and seq % TOK_TILE == 0 and seq // GRID_W >= WIN_H
    h = x.reshape(b * seq, dm)
    for c in layer_consts:
        h = _encoder_layer(h, b, seq, c)
    return h.reshape(b, seq, dm)


def kernel(x_prompt, x_sample, w_in, b_gate, rpb, mu_prev, mu_next, w0, w2, a0, a2, g2, k_k, k_a, r_k, ln_w, ln_b, w_proj_a, w_proj_b, w_out, w_up, w_down, norm_pre_mix, norm_post_mix, norm_pre_ffn, norm_post_ffn):
    stacked = (w_in, b_gate, rpb, mu_prev, mu_next, w0, w2, a0, a2, g2, k_k, k_a, r_k,
               ln_w, ln_b, w_proj_a, w_proj_b, w_out, w_up, w_down,
               norm_pre_mix, norm_post_mix, norm_pre_ffn, norm_post_ffn)
    depth = w_in.shape[0]
    layer_consts = [_layer_consts(*[p[i] for p in stacked]) for i in range(depth)]
    return (_trunk(x_prompt, layer_consts), _trunk(x_sample, layer_consts))
```

```python
import functools
import math

import jax
import jax.numpy as jnp
from jax import lax
from jax.experimental import pallas as pl
from jax.experimental.pallas import tpu as pltpu

F32 = jnp.float32
BF16 = jnp.bfloat16

D_MODEL = 1024
GRID_W = 64
WIN_H = 8
WIN_W = 16
NA_HEADS = 8
HEAD_DIM = 64
D_NA = NA_HEADS * HEAD_DIM
RWKV_HEADS = 8
D_RWKV = RWKV_HEADS * HEAD_DIM
DECAY_LORA = 64
ICLR_LORA = 64
GATE_LORA = 128
N_DIR = 2
D_FF = 4 * D_MODEL
D_SHIFT = 3 * D_RWKV + N_DIR * DECAY_LORA + N_DIR * ICLR_LORA + GATE_LORA
RMS_EPS = 1e-6
GN_EPS = 64e-5
KK_EPS = 1e-12
NEG_BIG = -1e30

CHUNK = 64
assert CHUNK == HEAD_DIM
PAIR = 2 * HEAD_DIM
N_PAIR = D_RWKV // PAIR
TOK_TILE = 512
HALO = 8
ROWS_PER_TILE = TOK_TILE // GRID_W
CHUNKS_PER_TILE = TOK_TILE // CHUNK
NA_ROWS_PER_ITER = 4
SCAN_GROUP = 4
V7X_VMEM_LIMIT = 56 * 1024 * 1024

_NT = (((1,), (1,)), ((), ()))


def _dot(a, b):
    return jnp.dot(a, b, preferred_element_type=F32)


def _sigmoid(x):
    return 0.5 * jnp.tanh(0.5 * x) + 0.5


def _params(sem):
    return pltpu.CompilerParams(dimension_semantics=sem, vmem_limit_bytes=V7X_VMEM_LIMIT)


def _const_spec(shape):
    nd = len(shape)
    return pl.BlockSpec(shape, lambda *_: (0,) * nd, pipeline_mode=pl.Buffered(1))


def _inproj_body(x_ref, xp_ref, xn_ref, g_ref, wqkv_ref, wsh_ref, wgt_ref, bg_ref,
                 mu0_ref, mup_ref, mun_ref, q_ref, k_ref, v_ref, sh_ref, gt_ref, *, tiles_per_seq):
    tm = TOK_TILE
    pos = lax.rem(pl.program_id(0), tiles_per_seq)
    xe = jnp.concatenate([xp_ref[...], x_ref[...], xn_ref[...]], axis=0)
    ms = jnp.mean(xe * xe, axis=-1, keepdims=True)
    he = xe * lax.rsqrt(ms + RMS_EPS) * g_ref[...]
    rowe = lax.broadcasted_iota(jnp.int32, (tm + 2 * HALO, 1), 0)
    inside = (((rowe >= HALO) | (pos != 0))
              & ((rowe < tm + HALO) | (pos != tiles_per_seq - 1)))
    he = jnp.where(inside, he, 0.0).astype(BF16)
    main = slice(HALO, tm + HALO)
    qkv = _dot(he, wqkv_ref[...])[main]
    q_ref[...] = (qkv[:, 0:D_NA] * (HEAD_DIM ** -0.5)).astype(BF16)
    k_ref[...] = qkv[:, D_NA:2 * D_NA].astype(BF16)
    v_ref[...] = qkv[:, 2 * D_NA:3 * D_NA].astype(BF16)
    pe = _dot(he, wsh_ref[...])
    sh_ref[...] = (mu0_ref[...] * pe[main] + mup_ref[...] * pe[HALO - 1:tm + HALO - 1]
                   + mun_ref[...] * pe[HALO + 1:tm + HALO + 1])
    gt_ref[...] = _sigmoid(_dot(he, wgt_ref[...])[main] + bg_ref[...]).astype(BF16)


def _inproj(x, seq, g, wqkv, wsh, wgt, bg, mu0, mup, mun):
    t = x.shape[0]
    tm = TOK_TILE
    nh = tm // HALO
    last = t // HALO - 1
    row = lambda w: pl.BlockSpec((tm, w), lambda i: (i, 0))
    consts = [g, wqkv, wsh, wgt, bg, mu0, mup, mun]
    return pl.pallas_call(
        functools.partial(_inproj_body, tiles_per_seq=seq // tm),
        grid=(t // tm,),
        in_specs=[row(D_MODEL),
                  pl.BlockSpec((HALO, D_MODEL), lambda i: (jnp.maximum(i * nh - 1, 0), 0)),
                  pl.BlockSpec((HALO, D_MODEL), lambda i: (jnp.minimum((i + 1) * nh, last), 0))]
        + [_const_spec(c.shape) for c in consts],
        out_specs=[row(D_NA), row(D_NA), row(D_NA), row(D_SHIFT), row(2 * D_MODEL)],
        out_shape=[jax.ShapeDtypeStruct((t, D_NA), BF16)] * 3
        + [jax.ShapeDtypeStruct((t, D_SHIFT), F32), jax.ShapeDtypeStruct((t, 2 * D_MODEL), BF16)],
        compiler_params=_params(("parallel",)),
        name="inproj",
    )(x, x, x, *consts)


def _na_bias_table(rpb):
    c = jnp.arange(GRID_W)
    col_start = jnp.clip(c - WIN_W // 2, 0, GRID_W - WIN_W)
    col_ok = (c[None, :] >= col_start[:, None]) & (c[None, :] < col_start[:, None] + WIN_W)
    col_idx = jnp.clip(c[None, :] - c[:, None] + (WIN_W - 1), 0, 2 * WIN_W - 2)
    onehot = (col_idx[:, :, None] == jnp.arange(2 * WIN_W - 1)[None, None, :]).astype(F32)
    cols = jnp.einsum("hdc,qkc->hqdk", rpb.astype(F32), onehot,
                      precision=lax.Precision.HIGHEST)
    cols = jnp.where(col_ok[None, :, None, :], cols, NEG_BIG)
    tab = jnp.stack([cols[:, :, WIN_H - 1 - e:2 * WIN_H - 1 - e, :] for e in range(WIN_H)])
    return tab.reshape(WIN_H, NA_HEADS, GRID_W, WIN_H * GRID_W)


def _natten_body(q_ref, kp_ref, kc_ref, kn_ref, vp_ref, vc_ref, vn_ref, bias_ref, o_ref,
                 kwin, vwin, *, rows):
    j = pl.program_id(1)
    kwin[0:TOK_TILE, :] = kp_ref[...]
    kwin[TOK_TILE:2 * TOK_TILE, :] = kc_ref[...]
    kwin[2 * TOK_TILE:3 * TOK_TILE, :] = kn_ref[...]
    vwin[0:TOK_TILE, :] = vp_ref[...]
    vwin[TOK_TILE:2 * TOK_TILE, :] = vc_ref[...]
    vwin[2 * TOK_TILE:3 * TOK_TILE, :] = vn_ref[...]
    first_half = lax.broadcasted_iota(jnp.int32, (GRID_W, PAIR), 1) < HEAD_DIM
    n_keys = WIN_H * GRID_W
    zero_q = jnp.zeros((GRID_W, PAIR), BF16)

    def rows_body(it, carry):
        work = []
        for rr in range(NA_ROWS_PER_ITER):
            jr = it * NA_ROWS_PER_ITER + rr
            r = j * ROWS_PER_TILE + jr
            rs = jnp.clip(r - WIN_H // 2, 0, rows - WIN_H)
            off = pl.multiple_of((rs - (j - 1) * ROWS_PER_TILE) * GRID_W, GRID_W)
            q0 = pl.multiple_of(jr * GRID_W, GRID_W)
            for p in range(N_PAIR):
                work.append((r - rs, off, q0, p, slice(p * PAIR, (p + 1) * PAIR)))
        scores = []
        for e, off, q0, p, ls in work:
            qp = q_ref[pl.ds(q0, GRID_W), ls]
            q2 = jnp.concatenate([jnp.where(first_half, qp, zero_q),
                                  jnp.where(first_half, zero_q, qp)], axis=0)
            scores.append(lax.dot_general(q2, kwin[pl.ds(off, n_keys), ls], _NT,
                                          preferred_element_type=F32))
        probs, sums = [], []
        for (e, off, q0, p, ls), s2 in zip(work, scores):
            pr2 = []
            for hh in range(2):
                s = s2[hh * GRID_W:(hh + 1) * GRID_W] + bias_ref[e, 2 * p + hh]
                pr = jnp.exp(s - jnp.max(s, axis=-1, keepdims=True))
                sums.append(jnp.sum(pr, axis=-1, keepdims=True))
                pr2.append(pr.astype(BF16))
            probs.append(jnp.concatenate(pr2, axis=0))
        outs = [_dot(pr2, vwin[pl.ds(off, n_keys), ls]) for (e, off, q0, p, ls), pr2 in zip(work, probs)]
        for i, ((e, off, q0, p, ls), o2) in enumerate(zip(work, outs)):
            o0 = o2[0:GRID_W] / sums[2 * i]
            o1 = o2[GRID_W:] / sums[2 * i + 1]
            o_ref[pl.ds(q0, GRID_W), ls] = jnp.where(first_half, o0, o1).astype(BF16)
        return carry

    lax.fori_loop(0, ROWS_PER_TILE // NA_ROWS_PER_ITER, rows_body, 0)


def _natten(q, k, v, bias, b, seq):
    t = q.shape[0]
    rows = seq // GRID_W
    nblk = seq // TOK_TILE
    cur = lambda bi, j: (bi * nblk + j, 0)
    prev = lambda bi, j: (bi * nblk + jnp.maximum(j - 1, 0), 0)
    nxt = lambda bi, j: (bi * nblk + jnp.minimum(j + 1, nblk - 1), 0)
    blk = lambda im: pl.BlockSpec((TOK_TILE, D_NA), im)
    return pl.pallas_call(
        functools.partial(_natten_body, rows=rows),
        grid=(b, nblk),
        in_specs=[blk(cur), blk(prev), blk(cur), blk(nxt), blk(prev), blk(cur), blk(nxt),
                  _const_spec(bias.shape)],
        out_specs=blk(cur),
        out_shape=jax.ShapeDtypeStruct((t, D_NA), BF16),
        scratch_shapes=[pltpu.VMEM((3 * TOK_TILE, D_NA), BF16),
                        pltpu.VMEM((3 * TOK_TILE, D_NA), BF16)],
        compiler_params=_params(("parallel", "parallel")),
        name="natten",
    )(q, k, k, k, v, v, v, bias)


def _split2_dot(tri, x):
    x0 = x.astype(BF16)
    x1 = (x - x0.astype(F32)).astype(BF16)
    return _dot(tri, x0) + _dot(tri, x1)


def _prep_body(sh_ref, w0_ref, w2_ref, a0_ref, a2_ref, g2_ref,
               kkw_ref, ka_ref, rk_ref, seg_ref, tri_ref,
               rt_ref, kt_ref, bt_ref, at_ref, be_ref, ke_ref, v_ref, pend_ref, g_ref, bonus_ref):
    sh = sh_ref[...]
    r = sh[:, 0:D_RWKV]
    k = sh[:, D_RWKV:2 * D_RWKV]
    v = sh[:, 2 * D_RWKV:3 * D_RWKV]
    off = 3 * D_RWKV
    wl = sh[:, off:off + 2 * DECAY_LORA]
    al = sh[:, off + 2 * DECAY_LORA:off + 2 * DECAY_LORA + 2 * ICLR_LORA]
    gl = sh[:, off + 2 * DECAY_LORA + 2 * ICLR_LORA:]

    zw = w0_ref[...] + _dot(jnp.tanh(wl).astype(BF16), w2_ref[...])
    za = a0_ref[...] + _dot(al.astype(BF16), a2_ref[...])
    g_ref[...] = _dot(_sigmoid(gl).astype(BF16), g2_ref[...]).astype(BF16)
    lw = (-math.exp(-0.5) * math.log2(math.e)) * _sigmoid(zw)
    iclr = _sigmoid(za)

    kkv = k * kkw_ref[...]
    kkn = kkv * lax.rsqrt(_dot((kkv * kkv).astype(BF16), seg_ref[...]) + KK_EPS)
    v_ref[...] = v.astype(BF16)

    bonus_acc = jnp.zeros_like(r)
    for d in range(N_DIR):
        ds_ = slice(d * D_RWKV, (d + 1) * D_RWKV)
        icl = iclr[:, ds_]
        kd = k * (1.0 + (icl - 1.0) * ka_ref[...])
        bd = kkn * icl
        bonus_acc = bonus_acc + r * kd * rk_ref[...]
        lwd = lw[:, ds_]
        tri = tri_ref[d]
        for c in range(CHUNKS_PER_TILE):
            cs = slice(c * CHUNK, (c + 1) * CHUNK)
            x = lwd[cs]
            lp = _split2_dot(tri, x)
            lend = lp[CHUNK - 1:CHUNK] if d == 0 else lp[0:1]
            pw = jnp.exp2(lp)
            pinv = jnp.exp2(-lp)
            pex = jnp.exp2(lp - x)
            pe = jnp.exp2(lend - lp)
            rt_ref[d, cs, :] = (r[cs] * pw).astype(BF16)
            kt_ref[d, cs, :] = (kd[cs] * pinv).astype(BF16)
            bt_ref[d, cs, :] = (bd[cs] * pinv).astype(BF16)
            at_ref[d, cs, :] = (-kkn[cs] * pex).astype(BF16)
            be_ref[d, cs, :] = (bd[cs] * pe).astype(BF16)
            ke_ref[d, cs, :] = (kd[cs] * pe).astype(BF16)
            pend_ref[d, c] = jnp.exp2(lend)
    bonus_ref[...] = (_dot(bonus_acc.astype(BF16), seg_ref[...]) * v).astype(BF16)


def _prep(sh, w0, w2bd, a0, a2bd, g2, kkw, ka, rk, seg, tri):
    t = sh.shape[0]
    tb = TOK_TILE
    row = lambda w: pl.BlockSpec((tb, w), lambda i: (i, 0))
    drow = pl.BlockSpec((N_DIR, tb, D_RWKV), lambda i: (0, i, 0))
    consts = [w0, w2bd, a0, a2bd, g2, kkw, ka, rk, seg, tri]
    bf = jax.ShapeDtypeStruct((N_DIR, t, D_RWKV), BF16)
    return pl.pallas_call(
        _prep_body,
        grid=(t // tb,),
        in_specs=[row(D_SHIFT)] + [_const_spec(c.shape) for c in consts],
        out_specs=[drow] * 6 + [row(D_RWKV),
                                pl.BlockSpec((N_DIR, CHUNKS_PER_TILE, 1, D_RWKV), lambda i: (0, i, 0, 0)),
                                row(D_RWKV), row(D_RWKV)],
        out_shape=[bf] * 6 + [jax.ShapeDtypeStruct((t, D_RWKV), BF16),
                              jax.ShapeDtypeStruct((N_DIR, t // CHUNK, 1, D_RWKV), F32),
                              jax.ShapeDtypeStruct((t, D_RWKV), BF16),
                              jax.ShapeDtypeStruct((t, D_RWKV), BF16)],
        compiler_params=_params(("parallel",)),
        name="rwkv_prep",
    )(sh, *consts)


def _scan_factors(d, rt_ref, kt_ref, bt_ref, at_ref, be_ref, ke_ref, v_ref, pend_ref,
                  y_ref, q_scr, m_scr, d_scr):
    sign = 1 - 2 * d
    rowi2 = lax.broadcasted_iota(jnp.int32, (CHUNK, 2 * PAIR), 0)
    lane2 = lax.broadcasted_iota(jnp.int32, (CHUNK, 2 * PAIR), 1)
    diff = (rowi2 - (lane2 & (HEAD_DIM - 1))) * sign
    strict2 = diff > 0
    incl2 = diff >= 0
    low2 = (lane2 & HEAD_DIM) == 0
    rowi = lax.broadcasted_iota(jnp.int32, (CHUNK, PAIR), 0)
    lane = lax.broadcasted_iota(jnp.int32, (CHUNK, PAIR), 1)
    first_half = lane < HEAD_DIM
    eye_lo = (lane == rowi).astype(F32)
    eye_hi = (lane - HEAD_DIM == rowi).astype(F32)
    rowp = lax.broadcasted_iota(jnp.int32, (PAIR, PAIR), 0)
    lanep = lax.broadcasted_iota(jnp.int32, (PAIR, PAIR), 1)
    same_half = (rowp < HEAD_DIM) == (lanep < HEAD_DIM)
    eye_pair = rowp == lanep
    zero = jnp.zeros((CHUNK, PAIR), BF16)
    zero2 = jnp.zeros((CHUNK, 2 * PAIR), BF16)
    cat0 = lambda xs: jnp.concatenate(xs, axis=0)
    cat1 = lambda xs: jnp.concatenate(xs, axis=1)
    sel = lambda a, hh: jnp.where(first_half, a, zero) if hh == 0 else jnp.where(first_half, zero, a)
    lo2 = lambda a: jnp.where(low2, a, zero2)
    hi2 = lambda a: jnp.where(low2, zero2, a)

    def group_body(gi, carry):
        prs = []
        for cc in range(SCAN_GROUP):
            c = gi * SCAN_GROUP + cc
            t0 = pl.multiple_of(c * CHUNK, CHUNK)
            for p in range(N_PAIR):
                ls = slice(p * PAIR, (p + 1) * PAIR)
                ld = lambda ref: ref[0, pl.ds(t0, CHUNK), ls]
                prs.append(dict(c=c, t0=t0, p=p, ls=ls, rt=ld(rt_ref), kt=ld(kt_ref), bt=ld(bt_ref),
                                at=ld(at_ref), be=ld(be_ref), ke=ld(ke_ref),
                                v=v_ref[pl.ds(t0, CHUNK), ls], pend=pend_ref[0, c, :, ls]))
        for pr in prs:
            rhs = cat0([sel(pr["bt"], 0), sel(pr["kt"], 0), sel(pr["kt"], 1), sel(pr["bt"], 1)])
            aa = lax.dot_general(cat0([pr["at"], pr["rt"]]), rhs, _NT, preferred_element_type=F32)
            top = jnp.where(strict2, aa[0:CHUNK], 0.0)
            pr["top"] = top.astype(BF16)
            pr["bot"] = jnp.where(incl2, aa[CHUNK:], 0.0).astype(BF16)
            pr["x0"] = jnp.where(first_half, top[:, :PAIR], eye_hi)
            pr["x1"] = jnp.where(first_half, eye_lo, top[:, PAIR:])
        for pr in prs:
            v = pr["v"]
            pr["akv"] = _dot(pr["top"], cat0([zero2, cat1([v, zero]), cat1([zero, v]), zero2])).astype(BF16)
        for _ in range(int(math.log2(CHUNK))):
            res = []
            for pr in prs:
                x0b, x1b = pr["x0"].astype(BF16), pr["x1"].astype(BF16)
                res.append(_dot(jnp.where(first_half, x0b, x1b),
                                cat0([cat1([x0b, zero]), cat1([zero, x1b])])))
            for pr, r in zip(prs, res):
                pr["x0"] = jnp.where(first_half, r[:, :PAIR], pr["x0"] + r[:, :PAIR])
                pr["x1"] = jnp.where(first_half, pr["x1"] + r[:, PAIR:], r[:, PAIR:])
        for pr in prs:
            w1 = cat1([sel(pr["at"], 1), sel(pr["akv"][:, PAIR:], 1)])
            w0 = cat1([sel(pr["at"], 0), sel(pr["akv"][:, :PAIR], 0)])
            z = _dot((pr["x0"] + pr["x1"]).astype(BF16), cat0([w1, w0]))
            pr["au"] = z.astype(BF16)
            pr["zv"] = cat1([zero, pr["v"]])
        for pr in prs:
            au, zv = pr["au"], pr["zv"]
            pr["o"] = _dot(pr["bot"], cat0([lo2(au), lo2(zv), hi2(zv), hi2(au)]))
            pr["md"] = _dot(cat0([pr["be"], pr["ke"]]).T, cat0([au, zv]))
        for pr in prs:
            o, md, t0, ls = pr["o"], pr["md"], pr["t0"], pr["ls"]
            q_scr[pl.ds(t0, CHUNK), ls] = (pr["rt"].astype(F32) + o[:, :PAIR]).astype(BF16)
            y_ref[pl.ds(t0, CHUNK), ls] = o[:, PAIR:]
            mbd = jnp.where(eye_pair, pr["pend"], 0.0) + jnp.where(same_half, md[:, :PAIR], 0.0)
            m_scr[pr["c"], pr["p"]] = mbd.astype(BF16)
            d_scr[pr["c"], pr["p"]] = jnp.where(same_half, md[:, PAIR:], 0.0)
        return carry

    lax.fori_loop(0, CHUNKS_PER_TILE // SCAN_GROUP, group_body, 0)


def _scan_body(*refs):
    n_in = 8
    ins = (refs[0:n_in], refs[n_in:2 * n_in])
    ys = refs[2 * n_in:2 * n_in + 2]
    h_scr, q_scr, m_scr, d_scr = refs[2 * n_in + 2:]

    @pl.when(pl.program_id(1) == 0)
    def _():
        h_scr[...] = jnp.zeros_like(h_scr)

    for d in range(N_DIR):
        _scan_factors(d, *ins[d], ys[d], q_scr.at[d], m_scr.at[d], d_scr.at[d])

    states = [h_scr[d, p] for d in range(N_DIR) for p in range(N_PAIR)]
    for ci in range(CHUNKS_PER_TILE):
        work = []
        for d in range(N_DIR):
            c = ci if d == 0 else CHUNKS_PER_TILE - 1 - ci
            for p in range(N_PAIR):
                work.append((d, c, slice(c * CHUNK, (c + 1) * CHUNK), p, slice(p * PAIR, (p + 1) * PAIR)))
        hbs = [h.astype(BF16) for h in states]
        states = [_dot(m_scr[d, c, p], hb) + d_scr[d, c, p] for (d, c, ts, p, ls), hb in zip(work, hbs)]
        yd = [_dot(q_scr[d, ts, ls], hb) for (d, c, ts, p, ls), hb in zip(work, hbs)]
        for (d, c, ts, p, ls), y in zip(work, yd):
            ys[d][ts, ls] = ys[d][ts, ls] + y
    for i, h in enumerate(states):
        h_scr[i // N_PAIR, i % N_PAIR] = h


def _scan(rt, kt, bt, at, be, ke, v, pend, b, seq):
    t = v.shape[0]
    tc = TOK_TILE
    nblk = seq // tc
    tok = (lambda bi, j: bi * nblk + j, lambda bi, j: bi * nblk + nblk - 1 - j)
    in_specs, args = [], []
    for d in range(N_DIR):
        dblk = pl.BlockSpec((1, tc, D_RWKV), lambda bi, j, d=d: (d, tok[d](bi, j), 0))
        in_specs += [dblk] * 6 + [
            pl.BlockSpec((tc, D_RWKV), lambda bi, j, d=d: (tok[d](bi, j), 0)),
            pl.BlockSpec((1, CHUNKS_PER_TILE, 1, D_RWKV), lambda bi, j, d=d: (d, tok[d](bi, j), 0, 0))]
        args += [rt, kt, bt, at, be, ke, v, pend]
    return pl.pallas_call(
        _scan_body,
        grid=(b, nblk),
        in_specs=in_specs,
        out_specs=[pl.BlockSpec((tc, D_RWKV), lambda bi, j, d=d: (tok[d](bi, j), 0)) for d in range(N_DIR)],
        out_shape=[jax.ShapeDtypeStruct((t, D_RWKV), F32)] * N_DIR,
        scratch_shapes=[pltpu.VMEM((N_DIR, N_PAIR, PAIR, PAIR), F32),
                        pltpu.VMEM((N_DIR, tc, D_RWKV), BF16),
                        pltpu.VMEM((N_DIR, CHUNKS_PER_TILE, N_PAIR, PAIR, PAIR), BF16),
                        pltpu.VMEM((N_DIR, CHUNKS_PER_TILE, N_PAIR, PAIR, PAIR), F32)],
        compiler_params=_params(("parallel", "arbitrary")),
        name="rwkv_scan",
    )(*args)


def _merge_body(y0_ref, y1_ref, bonus_ref, g_ref, ona_ref, gt_ref, x_ref, seg_ref, lnw_ref, lnb_ref,
                wpa_ref, wpb_ref, wout_ref, npost_ref, o_ref):
    inv_n = 1.0 / HEAD_DIM
    y = y0_ref[...] + y1_ref[...]
    mean = _dot(y.astype(BF16), seg_ref[...]) * inv_n
    yc = y - mean
    var = _dot((yc * yc).astype(BF16), seg_ref[...]) * inv_n
    yn = yc * lax.rsqrt(var + GN_EPS) * lnw_ref[...] + lnb_ref[...]
    orw = ((yn + bonus_ref[...]) * g_ref[...]).astype(BF16)
    gates = gt_ref[...]
    merged = (gates[:, :D_MODEL] * _dot(ona_ref[...], wpa_ref[...])
              + gates[:, D_MODEL:] * _dot(orw, wpb_ref[...]))
    z = _dot(merged.astype(BF16), wout_ref[...])
    ms = jnp.mean(z * z, axis=-1, keepdims=True)
    o_ref[...] = x_ref[...] + z * lax.rsqrt(ms + RMS_EPS) * npost_ref[...]


def _merge(y0, y1, bonus, g, ona, gates, x, seg, lnw, lnb, wpa, wpb, wout, npost):
    t = x.shape[0]
    tm = TOK_TILE
    row = lambda w: pl.BlockSpec((tm, w), lambda i: (i, 0))
    consts = [seg, lnw, lnb, wpa, wpb, wout, npost]
    return pl.pallas_call(
        _merge_body,
        grid=(t // tm,),
        in_specs=[row(D_RWKV), row(D_RWKV), row(D_RWKV), row(D_RWKV), row(D_NA), row(2 * D_MODEL), row(D_MODEL)]
        + [_const_spec(c.shape) for c in consts],
        out_specs=row(D_MODEL),
        out_shape=jax.ShapeDtypeStruct((t, D_MODEL), F32),
        compiler_params=_params(("parallel",)),
        name="merge",
    )(y0, y1, bonus, g, ona, gates, x, *consts)


def _ffn_body(x_ref, npre_ref, wup_ref, wdn_ref, npost_ref, o_ref):
    x = x_ref[...]
    ms = jnp.mean(x * x, axis=-1, keepdims=True)
    h = (x * lax.rsqrt(ms + RMS_EPS) * npre_ref[...]).astype(BF16)
    u = jnp.maximum(_dot(h, wup_ref[...]), 0.0)
    f = _dot((u * u).astype(BF16), wdn_ref[...])
    ms = jnp.mean(f * f, axis=-1, keepdims=True)
    o_ref[...] = x + f * lax.rsqrt(ms + RMS_EPS) * npost_ref[...]


def _ffn(x, npre, wup, wdn, npost):
    t = x.shape[0]
    tm = TOK_TILE
    row = pl.BlockSpec((tm, D_MODEL), lambda i: (i, 0))
    consts = [npre, wup, wdn, npost]
    return pl.pallas_call(
        _ffn_body,
        grid=(t // tm,),
        in_specs=[row] + [_const_spec(c.shape) for c in consts],
        out_specs=row,
        out_shape=jax.ShapeDtypeStruct((t, D_MODEL), F32),
        compiler_params=_params(("parallel",)),
        name="ffn",
    )(x, *consts)


def _block_diag2(w):
    z = jnp.zeros_like(w[0])
    return jnp.concatenate([jnp.concatenate([w[0], z], axis=1),
                            jnp.concatenate([z, w[1]], axis=1)], axis=0)


def _layer_consts(w_in, b_gate, rpb, mu_prev, mu_next, w0, w2, a0, a2, g2, k_k, k_a, r_k,
                  ln_w, ln_b, w_proj_a, w_proj_b, w_out, w_up, w_down,
                  norm_pre_mix, norm_post_mix, norm_pre_ffn, norm_post_ffn):
    rowv = lambda a: a.reshape(1, -1).astype(F32)
    head = jnp.arange(D_RWKV) // HEAD_DIM
    idx = jnp.arange(CHUNK)
    tril = (idx[:, None] >= idx[None, :])
    return dict(
        wqkv=w_in[:, :3 * D_NA].astype(BF16),
        wsh=w_in[:, 3 * D_NA:3 * D_NA + D_SHIFT].astype(BF16),
        wgt=w_in[:, 3 * D_NA + D_SHIFT:].astype(BF16),
        bg=rowv(b_gate), bias=_na_bias_table(rpb),
        mu0=rowv(1.0 - mu_prev - mu_next), mup=rowv(mu_prev), mun=rowv(mu_next),
        w0=rowv(w0), w2bd=_block_diag2(w2).astype(BF16),
        a0=rowv(a0), a2bd=_block_diag2(a2).astype(BF16),
        g2=g2.astype(BF16), kkw=rowv(k_k), ka=rowv(k_a), rk=rowv(r_k),
        seg=(head[:, None] == head[None, :]).astype(BF16),
        tri=jnp.stack([tril, tril.T]).astype(BF16),
        lnw=rowv(ln_w), lnb=rowv(ln_b),
        wpa=w_proj_a.astype(BF16), wpb=w_proj_b.astype(BF16), wout=w_out.astype(BF16),
        wup=w_up.astype(BF16), wdn=w_down.astype(BF16),
        npre_mix=rowv(norm_pre_mix), npost_mix=rowv(norm_post_mix),
        npre_ffn=rowv(norm_pre_ffn), npost_ffn=rowv(norm_post_ffn),
    )


def _encoder_layer(x, b, seq, c):
    q, k, v, sh, gates = _inproj(x, seq, c["npre_mix"], c["wqkv"], c["wsh"], c["wgt"], c["bg"],
                                 c["mu0"], c["mup"], c["mun"])
    ona = _natten(q, k, v, c["bias"], b, seq)
    rt, kt, bt, at, be, ke, vr, pend, g, bonus = _prep(
        sh, c["w0"], c["w2bd"], c["a0"], c["a2bd"], c["g2"],
        c["kkw"], c["ka"], c["rk"], c["seg"], c["tri"])
    y0, y1 = _scan(rt, kt, bt, at, be, ke, vr, pend, b, seq)
    x1 = _merge(y0, y1, bonus, g, ona, gates, x, c["seg"], c["lnw"], c["lnb"],
                c["wpa"], c["wpb"], c["wout"], c["npost_mix"])
    return _ffn(x1, c["npre_ffn"], c["wup"], c["wdn"], c["npost_ffn"])


def _trunk(x, layer_consts):
    b, seq, dm = x.shape
    assert dm == D_MODEL and seq % TOK_TILE == 0 and seq // GRID_W >= WIN_H
    h = x.reshape(b * seq, dm)
    for c in layer_consts:
        h = _encoder_layer(h, b, seq, c)
    return h.reshape(b, seq, dm)


def kernel(x_prompt, x_sample, w_in, b_gate, rpb, mu_prev, mu_next, w0, w2, a0, a2, g2, k_k, k_a, r_k, ln_w, ln_b, w_proj_a, w_proj_b, w_out, w_up, w_down, norm_pre_mix, norm_post_mix, norm_pre_ffn, norm_post_ffn):
    stacked = (w_in, b_gate, rpb, mu_prev, mu_next, w0, w2, a0, a2, g2, k_k, k_a, r_k,
               ln_w, ln_b, w_proj_a, w_proj_b, w_out, w_up, w_down,
               norm_pre_mix, norm_post_mix, norm_pre_ffn, norm_post_ffn)
    depth = w_in.shape[0]
    layer_consts = [_layer_consts(*[p[i] for p in stacked]) for i in range(depth)]
    return (_trunk(x_prompt, layer_consts), _trunk(x_sample, layer_consts))
```

```python
import functools
import math

import jax
import jax.numpy as jnp
from jax import lax
from jax.experimental import pallas as pl
from jax.experimental.pallas import tpu as pltpu

F32 = jnp.float32
BF16 = jnp.bfloat16

D_MODEL = 1024
GRID_W = 64
WIN_H = 8
WIN_W = 16
NA_HEADS = 8
HEAD_DIM = 64
D_NA = NA_HEADS * HEAD_DIM
RWKV_HEADS = 8
D_RWKV = RWKV_HEADS * HEAD_DIM
DECAY_LORA = 64
ICLR_LORA = 64
GATE_LORA = 128
N_DIR = 2
D_FF = 4 * D_MODEL
D_SHIFT = 3 * D_RWKV + N_DIR * DECAY_LORA + N_DIR * ICLR_LORA + GATE_LORA
RMS_EPS = 1e-6
GN_EPS = 64e-5
KK_EPS = 1e-12
NEG_BIG = -1e30

CHUNK = 64
assert CHUNK == HEAD_DIM
PAIR = 2 * HEAD_DIM
N_PAIR = D_RWKV // PAIR
TOK_TILE = 512
HALO = 8
ROWS_PER_TILE = TOK_TILE // GRID_W
CHUNKS_PER_TILE = TOK_TILE // CHUNK
NA_ROWS_PER_ITER = 2
SCAN_GROUP = 4
V7X_VMEM_LIMIT = 56 * 1024 * 1024

_NT = (((1,), (1,)), ((), ()))


def _dot(a, b):
    return jnp.dot(a, b, preferred_element_type=F32)


def _sigmoid(x):
    return 0.5 * jnp.tanh(0.5 * x) + 0.5


def _params(sem):
    return pltpu.CompilerParams(dimension_semantics=sem, vmem_limit_bytes=V7X_VMEM_LIMIT)


def _const_spec(shape):
    nd = len(shape)
    return pl.BlockSpec(shape, lambda *_: (0,) * nd, pipeline_mode=pl.Buffered(1))


def _inproj_body(x_ref, xp_ref, xn_ref, g_ref, wqkv_ref, wsh_ref, wgt_ref, bg_ref,
                 mu0_ref, mup_ref, mun_ref, q_ref, k_ref, v_ref, sh_ref, gt_ref, *, tiles_per_seq):
    tm = TOK_TILE
    pos = lax.rem(pl.program_id(0), tiles_per_seq)
    xe = jnp.concatenate([xp_ref[...], x_ref[...], xn_ref[...]], axis=0)
    ms = jnp.mean(xe * xe, axis=-1, keepdims=True)
    he = xe * lax.rsqrt(ms + RMS_EPS) * g_ref[...]
    rowe = lax.broadcasted_iota(jnp.int32, (tm + 2 * HALO, 1), 0)
    inside = (((rowe >= HALO) | (pos != 0))
              & ((rowe < tm + HALO) | (pos != tiles_per_seq - 1)))
    he = jnp.where(inside, he, 0.0).astype(BF16)
    main = slice(HALO, tm + HALO)
    qkv = _dot(he, wqkv_ref[...])[main]
    q_ref[...] = (qkv[:, 0:D_NA] * (HEAD_DIM ** -0.5)).astype(BF16)
    k_ref[...] = qkv[:, D_NA:2 * D_NA].astype(BF16)
    v_ref[...] = qkv[:, 2 * D_NA:3 * D_NA].astype(BF16)
    pe = _dot(he, wsh_ref[...])
    sh_ref[...] = (mu0_ref[...] * pe[main] + mup_ref[...] * pe[HALO - 1:tm + HALO - 1]
                   + mun_ref[...] * pe[HALO + 1:tm + HALO + 1])
    gt_ref[...] = _sigmoid(_dot(he, wgt_ref[...])[main] + bg_ref[...]).astype(BF16)


def _inproj(x, seq, g, wqkv, wsh, wgt, bg, mu0, mup, mun):
    t = x.shape[0]
    tm = TOK_TILE
    nh = tm // HALO
    last = t // HALO - 1
    row = lambda w: pl.BlockSpec((tm, w), lambda i: (i, 0))
    consts = [g, wqkv, wsh, wgt, bg, mu0, mup, mun]
    return pl.pallas_call(
        functools.partial(_inproj_body, tiles_per_seq=seq // tm),
        grid=(t // tm,),
        in_specs=[row(D_MODEL),
                  pl.BlockSpec((HALO, D_MODEL), lambda i: (jnp.maximum(i * nh - 1, 0), 0)),
                  pl.BlockSpec((HALO, D_MODEL), lambda i: (jnp.minimum((i + 1) * nh, last), 0))]
        + [_const_spec(c.shape) for c in consts],
        out_specs=[row(D_NA), row(D_NA), row(D_NA), row(D_SHIFT), row(2 * D_MODEL)],
        out_shape=[jax.ShapeDtypeStruct((t, D_NA), BF16)] * 3
        + [jax.ShapeDtypeStruct((t, D_SHIFT), F32), jax.ShapeDtypeStruct((t, 2 * D_MODEL), BF16)],
        compiler_params=_params(("parallel",)),
        name="inproj",
    )(x, x, x, *consts)


def _na_bias_table(rpb):
    c = jnp.arange(GRID_W)
    col_start = jnp.clip(c - WIN_W // 2, 0, GRID_W - WIN_W)
    col_ok = (c[None, :] >= col_start[:, None]) & (c[None, :] < col_start[:, None] + WIN_W)
    col_idx = jnp.clip(c[None, :] - c[:, None] + (WIN_W - 1), 0, 2 * WIN_W - 2)
    onehot = (col_idx[:, :, None] == jnp.arange(2 * WIN_W - 1)[None, None, :]).astype(F32)
    cols = jnp.einsum("hdc,qkc->hqdk", rpb.astype(F32), onehot,
                      precision=lax.Precision.HIGHEST)
    cols = jnp.where(col_ok[None, :, None, :], cols, NEG_BIG)
    tab = jnp.stack([cols[:, :, WIN_H - 1 - e:2 * WIN_H - 1 - e, :] for e in range(WIN_H)])
    return tab.reshape(WIN_H, NA_HEADS, GRID_W, WIN_H * GRID_W)


def _natten_body(q_ref, k_ref, v_ref, bias_ref, o_ref, *, rows):
    j = pl.program_id(1)
    first_half = lax.broadcasted_iota(jnp.int32, (GRID_W, PAIR), 1) < HEAD_DIM
    n_keys = WIN_H * GRID_W
    zero_q = jnp.zeros((GRID_W, PAIR), BF16)

    def problems(row_ids):
        work = []
        for jr in row_ids:
            r = j * ROWS_PER_TILE + jr
            rs = jnp.clip(r - WIN_H // 2, 0, rows - WIN_H)
            off = pl.multiple_of(rs * GRID_W, GRID_W)
            for p in range(N_PAIR):
                work.append((r - rs, off, jr * GRID_W, p, slice(p * PAIR, (p + 1) * PAIR)))
        return work

    def qk(work):
        scores = []
        for e, off, q0, p, ls in work:
            qp = q_ref[q0:q0 + GRID_W, ls]
            q2 = jnp.concatenate([jnp.where(first_half, qp, zero_q),
                                  jnp.where(first_half, zero_q, qp)], axis=0)
            scores.append(lax.dot_general(q2, k_ref[pl.ds(off, n_keys), ls], _NT,
                                          preferred_element_type=F32))
        return scores

    def softmax(work, scores):
        probs, sums = [], []
        for (e, off, q0, p, ls), s2 in zip(work, scores):
            pr2 = []
            for hh in range(2):
                s = s2[hh * GRID_W:(hh + 1) * GRID_W] + bias_ref[e, 2 * p + hh]
                pr = jnp.exp(s - jnp.max(s, axis=-1, keepdims=True))
                sums.append(jnp.sum(pr, axis=-1, keepdims=True))
                pr2.append(pr.astype(BF16))
            probs.append(jnp.concatenate(pr2, axis=0))
        return probs, sums

    def pv(work, probs, sums):
        outs = [_dot(pr2, v_ref[pl.ds(off, n_keys), ls]) for (e, off, q0, p, ls), pr2 in zip(work, probs)]
        for i, ((e, off, q0, p, ls), o2) in enumerate(zip(work, outs)):
            o0 = o2[0:GRID_W] / sums[2 * i]
            o1 = o2[GRID_W:] / sums[2 * i + 1]
            o_ref[q0:q0 + GRID_W, ls] = jnp.where(first_half, o0, o1).astype(BF16)

    groups = [problems(range(g, g + NA_ROWS_PER_ITER)) for g in range(0, ROWS_PER_TILE, NA_ROWS_PER_ITER)]
    scores = qk(groups[0])
    for gi, work in enumerate(groups):
        nxt_scores = qk(groups[gi + 1]) if gi + 1 < len(groups) else None
        probs, sums = softmax(work, scores)
        pv(work, probs, sums)
        scores = nxt_scores


def _natten(q, k, v, bias, b, seq):
    t = q.shape[0]
    rows = seq // GRID_W
    nblk = seq // TOK_TILE
    blk = pl.BlockSpec((TOK_TILE, D_NA), lambda bi, j: (bi * nblk + j, 0))
    whole = pl.BlockSpec((seq, D_NA), lambda bi, j: (bi, 0))
    return pl.pallas_call(
        functools.partial(_natten_body, rows=rows),
        grid=(b, nblk),
        in_specs=[blk, whole, whole, _const_spec(bias.shape)],
        out_specs=blk,
        out_shape=jax.ShapeDtypeStruct((t, D_NA), BF16),
        compiler_params=_params(("parallel", "arbitrary")),
        name="natten",
    )(q, k, v, bias)


def _split2_dot(tri, x):
    x0 = x.astype(BF16)
    x1 = (x - x0.astype(F32)).astype(BF16)
    return _dot(tri, x0) + _dot(tri, x1)


def _prep_body(sh_ref, w0_ref, w2_ref, a0_ref, a2_ref, g2_ref,
               kkw_ref, ka_ref, rk_ref, seg_ref, tri_ref,
               rt_ref, kt_ref, bt_ref, at_ref, be_ref, ke_ref, v_ref, pend_ref, g_ref, bonus_ref):
    sh = sh_ref[...]
    r = sh[:, 0:D_RWKV]
    k = sh[:, D_RWKV:2 * D_RWKV]
    v = sh[:, 2 * D_RWKV:3 * D_RWKV]
    off = 3 * D_RWKV
    wl = sh[:, off:off + 2 * DECAY_LORA]
    al = sh[:, off + 2 * DECAY_LORA:off + 2 * DECAY_LORA + 2 * ICLR_LORA]
    gl = sh[:, off + 2 * DECAY_LORA + 2 * ICLR_LORA:]

    zw = w0_ref[...] + _dot(jnp.tanh(wl).astype(BF16), w2_ref[...])
    za = a0_ref[...] + _dot(al.astype(BF16), a2_ref[...])
    g_ref[...] = _dot(_sigmoid(gl).astype(BF16), g2_ref[...]).astype(BF16)
    lw = (-math.exp(-0.5) * math.log2(math.e)) * _sigmoid(zw)
    iclr = _sigmoid(za)

    kkv = k * kkw_ref[...]
    kkn = kkv * lax.rsqrt(_dot((kkv * kkv).astype(BF16), seg_ref[...]) + KK_EPS)
    v_ref[...] = v.astype(BF16)

    bonus_acc = jnp.zeros_like(r)
    for d in range(N_DIR):
        ds_ = slice(d * D_RWKV, (d + 1) * D_RWKV)
        icl = iclr[:, ds_]
        kd = k * (1.0 + (icl - 1.0) * ka_ref[...])
        bd = kkn * icl
        bonus_acc = bonus_acc + r * kd * rk_ref[...]
        lwd = lw[:, ds_]
        tri = tri_ref[d]
        for c in range(CHUNKS_PER_TILE):
            cs = slice(c * CHUNK, (c + 1) * CHUNK)
            x = lwd[cs]
            lp = _split2_dot(tri, x)
            lend = lp[CHUNK - 1:CHUNK] if d == 0 else lp[0:1]
            pw = jnp.exp2(lp)
            pinv = jnp.exp2(-lp)
            pex = jnp.exp2(lp - x)
            pe = jnp.exp2(lend - lp)
            rt_ref[d, cs, :] = (r[cs] * pw).astype(BF16)
            kt_ref[d, cs, :] = (kd[cs] * pinv).astype(BF16)
            bt_ref[d, cs, :] = (bd[cs] * pinv).astype(BF16)
            at_ref[d, cs, :] = (-kkn[cs] * pex).astype(BF16)
            be_ref[d, cs, :] = (bd[cs] * pe).astype(BF16)
            ke_ref[d, cs, :] = (kd[cs] * pe).astype(BF16)
            pend_ref[d, c] = jnp.exp2(lend)
    bonus_ref[...] = (_dot(bonus_acc.astype(BF16), seg_ref[...]) * v).astype(BF16)


def _prep(sh, w0, w2bd, a0, a2bd, g2, kkw, ka, rk, seg, tri):
    t = sh.shape[0]
    tb = TOK_TILE
    row = lambda w: pl.BlockSpec((tb, w), lambda i: (i, 0))
    drow = pl.BlockSpec((N_DIR, tb, D_RWKV), lambda i: (0, i, 0))
    consts = [w0, w2bd, a0, a2bd, g2, kkw, ka, rk, seg, tri]
    bf = jax.ShapeDtypeStruct((N_DIR, t, D_RWKV), BF16)
    return pl.pallas_call(
        _prep_body,
        grid=(t // tb,),
        in_specs=[row(D_SHIFT)] + [_const_spec(c.shape) for c in consts],
        out_specs=[drow] * 6 + [row(D_RWKV),
                                pl.BlockSpec((N_DIR, CHUNKS_PER_TILE, 1, D_RWKV), lambda i: (0, i, 0, 0)),
                                row(D_RWKV), row(D_RWKV)],
        out_shape=[bf] * 6 + [jax.ShapeDtypeStruct((t, D_RWKV), BF16),
                              jax.ShapeDtypeStruct((N_DIR, t // CHUNK, 1, D_RWKV), F32),
                              jax.ShapeDtypeStruct((t, D_RWKV), BF16),
                              jax.ShapeDtypeStruct((t, D_RWKV), BF16)],
        compiler_params=_params(("parallel",)),
        name="rwkv_prep",
    )(sh, *consts)


def _scan_factors(d, rt_ref, kt_ref, bt_ref, at_ref, be_ref, ke_ref, v_ref, pend_ref,
                  y0_scr, q_scr, m_scr, d_scr):
    sign = 1 - 2 * d
    rowi2 = lax.broadcasted_iota(jnp.int32, (CHUNK, 2 * PAIR), 0)
    lane2 = lax.broadcasted_iota(jnp.int32, (CHUNK, 2 * PAIR), 1)
    diff = (rowi2 - (lane2 & (HEAD_DIM - 1))) * sign
    strict2 = diff > 0
    incl2 = diff >= 0
    low2 = (lane2 & HEAD_DIM) == 0
    rowi = lax.broadcasted_iota(jnp.int32, (CHUNK, PAIR), 0)
    lane = lax.broadcasted_iota(jnp.int32, (CHUNK, PAIR), 1)
    first_half = lane < HEAD_DIM
    eye_lo = (lane == rowi).astype(F32)
    eye_hi = (lane - HEAD_DIM == rowi).astype(F32)
    rowp = lax.broadcasted_iota(jnp.int32, (PAIR, PAIR), 0)
    lanep = lax.broadcasted_iota(jnp.int32, (PAIR, PAIR), 1)
    same_half = (rowp < HEAD_DIM) == (lanep < HEAD_DIM)
    eye_pair = rowp == lanep
    zero = jnp.zeros((CHUNK, PAIR), BF16)
    zero2 = jnp.zeros((CHUNK, 2 * PAIR), BF16)
    cat0 = lambda xs: jnp.concatenate(xs, axis=0)
    cat1 = lambda xs: jnp.concatenate(xs, axis=1)
    sel = lambda a, hh: jnp.where(first_half, a, zero) if hh == 0 else jnp.where(first_half, zero, a)
    lo2 = lambda a: jnp.where(low2, a, zero2)
    hi2 = lambda a: jnp.where(low2, zero2, a)

    def group_body(gi, carry):
        prs = []
        for cc in range(SCAN_GROUP):
            c = gi * SCAN_GROUP + cc
            t0 = pl.multiple_of(c * CHUNK, CHUNK)
            for p in range(N_PAIR):
                ls = slice(p * PAIR, (p + 1) * PAIR)
                ld = lambda ref: ref[0, pl.ds(t0, CHUNK), ls]
                prs.append(dict(c=c, t0=t0, p=p, ls=ls, rt=ld(rt_ref), kt=ld(kt_ref), bt=ld(bt_ref),
                                at=ld(at_ref), be=ld(be_ref), ke=ld(ke_ref),
                                v=v_ref[pl.ds(t0, CHUNK), ls], pend=pend_ref[0, c, :, ls]))
        for pr in prs:
            rhs = cat0([sel(pr["bt"], 0), sel(pr["kt"], 0), sel(pr["kt"], 1), sel(pr["bt"], 1)])
            aa = lax.dot_general(cat0([pr["at"], pr["rt"]]), rhs, _NT, preferred_element_type=F32)
            top = jnp.where(strict2, aa[0:CHUNK], 0.0)
            pr["top"] = top.astype(BF16)
            pr["bot"] = jnp.where(incl2, aa[CHUNK:], 0.0).astype(BF16)
            pr["x0"] = jnp.where(first_half, top[:, :PAIR], eye_hi)
            pr["x1"] = jnp.where(first_half, eye_lo, top[:, PAIR:])
        for pr in prs:
            v = pr["v"]
            pr["akv"] = _dot(pr["top"], cat0([zero2, cat1([v, zero]), cat1([zero, v]), zero2])).astype(BF16)
        for _ in range(int(math.log2(CHUNK))):
            res = []
            for pr in prs:
                x0b, x1b = pr["x0"].astype(BF16), pr["x1"].astype(BF16)
                res.append(_dot(jnp.where(first_half, x0b, x1b),
                                cat0([cat1([x0b, zero]), cat1([zero, x1b])])))
            for pr, r in zip(prs, res):
                pr["x0"] = jnp.where(first_half, r[:, :PAIR], pr["x0"] + r[:, :PAIR])
                pr["x1"] = jnp.where(first_half, pr["x1"] + r[:, PAIR:], r[:, PAIR:])
        for pr in prs:
            w1 = cat1([sel(pr["at"], 1), sel(pr["akv"][:, PAIR:], 1)])
            w0 = cat1([sel(pr["at"], 0), sel(pr["akv"][:, :PAIR], 0)])
            z = _dot((pr["x0"] + pr["x1"]).astype(BF16), cat0([w1, w0]))
            pr["au"] = z.astype(BF16)
            pr["zv"] = cat1([zero, pr["v"]])
        for pr in prs:
            au, zv = pr["au"], pr["zv"]
            pr["o"] = _dot(pr["bot"], cat0([lo2(au), lo2(zv), hi2(zv), hi2(au)]))
            pr["md"] = _dot(cat0([pr["be"], pr["ke"]]).T, cat0([au, zv]))
        for pr in prs:
            o, md, t0, ls = pr["o"], pr["md"], pr["t0"], pr["ls"]
            q_scr[pl.ds(t0, CHUNK), ls] = (pr["rt"].astype(F32) + o[:, :PAIR]).astype(BF16)
            y0_scr[pl.ds(t0, CHUNK), ls] = o[:, PAIR:]
            mbd = jnp.where(eye_pair, pr["pend"], 0.0) + jnp.where(same_half, md[:, :PAIR], 0.0)
            m_scr[pr["c"], pr["p"]] = mbd.astype(BF16)
            d_scr[pr["c"], pr["p"]] = jnp.where(same_half, md[:, PAIR:], 0.0)
        return carry

    lax.fori_loop(0, CHUNKS_PER_TILE // SCAN_GROUP, group_body, 0)


def _scan_body(*refs):
    n_in = 8
    ins = (refs[0:n_in], refs[n_in:2 * n_in])
    ys = refs[2 * n_in:2 * n_in + 2]
    h_scr, y0_scr, q_scr, m_scr, d_scr = refs[2 * n_in + 2:]

    @pl.when(pl.program_id(1) == 0)
    def _():
        h_scr[...] = jnp.zeros_like(h_scr)

    for d in range(N_DIR):
        _scan_factors(d, *ins[d], y0_scr.at[d], q_scr.at[d], m_scr.at[d], d_scr.at[d])

    states = [h_scr[d, p] for d in range(N_DIR) for p in range(N_PAIR)]
    for ci in range(CHUNKS_PER_TILE):
        work = []
        for d in range(N_DIR):
            c = ci if d == 0 else CHUNKS_PER_TILE - 1 - ci
            for p in range(N_PAIR):
                work.append((d, c, slice(c * CHUNK, (c + 1) * CHUNK), p, slice(p * PAIR, (p + 1) * PAIR)))
        hbs = [h.astype(BF16) for h in states]
        states = [_dot(m_scr[d, c, p], hb) + d_scr[d, c, p] for (d, c, ts, p, ls), hb in zip(work, hbs)]
        yd = [_dot(q_scr[d, ts, ls], hb) for (d, c, ts, p, ls), hb in zip(work, hbs)]
        for (d, c, ts, p, ls), y in zip(work, yd):
            ys[d][ts, ls] = (y0_scr[d, ts, ls] + y).astype(BF16)
    for i, h in enumerate(states):
        h_scr[i // N_PAIR, i % N_PAIR] = h


def _scan(rt, kt, bt, at, be, ke, v, pend, b, seq):
    t = v.shape[0]
    tc = TOK_TILE
    nblk = seq // tc
    tok = (lambda bi, j: bi * nblk + j, lambda bi, j: bi * nblk + nblk - 1 - j)
    in_specs, args = [], []
    for d in range(N_DIR):
        dblk = pl.BlockSpec((1, tc, D_RWKV), lambda bi, j, d=d: (d, tok[d](bi, j), 0))
        in_specs += [dblk] * 6 + [
            pl.BlockSpec((tc, D_RWKV), lambda bi, j, d=d: (tok[d](bi, j), 0)),
            pl.BlockSpec((1, CHUNKS_PER_TILE, 1, D_RWKV), lambda bi, j, d=d: (d, tok[d](bi, j), 0, 0))]
        args += [rt, kt, bt, at, be, ke, v, pend]
    return pl.pallas_call(
        _scan_body,
        grid=(b, nblk),
        in_specs=in_specs,
        out_specs=[pl.BlockSpec((tc, D_RWKV), lambda bi, j, d=d: (tok[d](bi, j), 0)) for d in range(N_DIR)],
        out_shape=[jax.ShapeDtypeStruct((t, D_RWKV), BF16)] * N_DIR,
        scratch_shapes=[pltpu.VMEM((N_DIR, N_PAIR, PAIR, PAIR), F32),
                        pltpu.VMEM((N_DIR, tc, D_RWKV), F32),
                        pltpu.VMEM((N_DIR, tc, D_RWKV), BF16),
                        pltpu.VMEM((N_DIR, CHUNKS_PER_TILE, N_PAIR, PAIR, PAIR), BF16),
                        pltpu.VMEM((N_DIR, CHUNKS_PER_TILE, N_PAIR, PAIR, PAIR), F32)],
        compiler_params=_params(("parallel", "arbitrary")),
        name="rwkv_scan",
    )(*args)


def _merge_body(y0_ref, y1_ref, bonus_ref, g_ref, ona_ref, gt_ref, x_ref, seg_ref, lnw_ref, lnb_ref,
                wpa_ref, wpb_ref, wout_ref, npost_ref, o_ref):
    inv_n = 1.0 / HEAD_DIM
    y = y0_ref[...].astype(F32) + y1_ref[...]
    mean = _dot(y.astype(BF16), seg_ref[...]) * inv_n
    yc = y - mean
    var = _dot((yc * yc).astype(BF16), seg_ref[...]) * inv_n
    yn = yc * lax.rsqrt(var + GN_EPS) * lnw_ref[...] + lnb_ref[...]
    orw = ((yn + bonus_ref[...]) * g_ref[...]).astype(BF16)
    gates = gt_ref[...]
    merged = (gates[:, :D_MODEL] * _dot(ona_ref[...], wpa_ref[...])
              + gates[:, D_MODEL:] * _dot(orw, wpb_ref[...]))
    z = _dot(merged.astype(BF16), wout_ref[...])
    ms = jnp.mean(z * z, axis=-1, keepdims=True)
    o_ref[...] = x_ref[...] + z * lax.rsqrt(ms + RMS_EPS) * npost_ref[...]


def _merge(y0, y1, bonus, g, ona, gates, x, seg, lnw, lnb, wpa, wpb, wout, npost):
    t = x.shape[0]
    tm = TOK_TILE
    row = lambda w: pl.BlockSpec((tm, w), lambda i: (i, 0))
    consts = [seg, lnw, lnb, wpa, wpb, wout, npost]
    return pl.pallas_call(
        _merge_body,
        grid=(t // tm,),
        in_specs=[row(D_RWKV), row(D_RWKV), row(D_RWKV), row(D_RWKV), row(D_NA), row(2 * D_MODEL), row(D_MODEL)]
        + [_const_spec(c.shape) for c in consts],
        out_specs=row(D_MODEL),
        out_shape=jax.ShapeDtypeStruct((t, D_MODEL), F32),
        compiler_params=_params(("parallel",)),
        name="merge",
    )(y0, y1, bonus, g, ona, gates, x, *consts)


def _ffn_body(x_ref, npre_ref, wup_ref, wdn_ref, npost_ref, o_ref):
    x = x_ref[...]
    ms = jnp.mean(x * x, axis=-1, keepdims=True)
    h = (x * lax.rsqrt(ms + RMS_EPS) * npre_ref[...]).astype(BF16)
    u = jnp.maximum(_dot(h, wup_ref[...]), 0.0)
    f = _dot((u * u).astype(BF16), wdn_ref[...])
    ms = jnp.mean(f * f, axis=-1, keepdims=True)
    o_ref[...] = x + f * lax.rsqrt(ms + RMS_EPS) * npost_ref[...]


def _ffn(x, npre, wup, wdn, npost):
    t = x.shape[0]
    tm = TOK_TILE
    row = pl.BlockSpec((tm, D_MODEL), lambda i: (i, 0))
    consts = [npre, wup, wdn, npost]
    return pl.pallas_call(
        _ffn_body,
        grid=(t // tm,),
        in_specs=[row] + [_const_spec(c.shape) for c in consts],
        out_specs=row,
        out_shape=jax.ShapeDtypeStruct((t, D_MODEL), F32),
        compiler_params=_params(("parallel",)),
        name="ffn",
    )(x, *consts)


def _block_diag2(w):
    z = jnp.zeros_like(w[0])
    return jnp.concatenate([jnp.concatenate([w[0], z], axis=1),
                            jnp.concatenate([z, w[1]], axis=1)], axis=0)


def _layer_consts(w_in, b_gate, rpb, mu_prev, mu_next, w0, w2, a0, a2, g2, k_k, k_a, r_k,
                  ln_w, ln_b, w_proj_a, w_proj_b, w_out, w_up, w_down,
                  norm_pre_mix, norm_post_mix, norm_pre_ffn, norm_post_ffn):
    rowv = lambda a: a.reshape(1, -1).astype(F32)
    head = jnp.arange(D_RWKV) // HEAD_DIM
    idx = jnp.arange(CHUNK)
    tril = (idx[:, None] >= idx[None, :])
    return dict(
        wqkv=w_in[:, :3 * D_NA].astype(BF16),
        wsh=w_in[:, 3 * D_NA:3 * D_NA + D_SHIFT].astype(BF16),
        wgt=w_in[:, 3 * D_NA + D_SHIFT:].astype(BF16),
        bg=rowv(b_gate), bias=_na_bias_table(rpb),
        mu0=rowv(1.0 - mu_prev - mu_next), mup=rowv(mu_prev), mun=rowv(mu_next),
        w0=rowv(w0), w2bd=_block_diag2(w2).astype(BF16),
        a0=rowv(a0), a2bd=_block_diag2(a2).astype(BF16),
        g2=g2.astype(BF16), kkw=rowv(k_k), ka=rowv(k_a), rk=rowv(r_k),
        seg=(head[:, None] == head[None, :]).astype(BF16),
        tri=jnp.stack([tril, tril.T]).astype(BF16),
        lnw=rowv(ln_w), lnb=rowv(ln_b),
        wpa=w_proj_a.astype(BF16), wpb=w_proj_b.astype(BF16), wout=w_out.astype(BF16),
        wup=w_up.astype(BF16), wdn=w_down.astype(BF16),
        npre_mix=rowv(norm_pre_mix), npost_mix=rowv(norm_post_mix),
        npre_ffn=rowv(norm_pre_ffn), npost_ffn=rowv(norm_post_ffn),
    )


def _encoder_layer(x, b, seq, c):
    q, k, v, sh, gates = _inproj(x, seq, c["npre_mix"], c["wqkv"], c["wsh"], c["wgt"], c["bg"],
                                 c["mu0"], c["mup"], c["mun"])
    ona = _natten(q, k, v, c["bias"], b, seq)
    rt, kt, bt, at, be, ke, vr, pend, g, bonus = _prep(
        sh, c["w0"], c["w2bd"], c["a0"], c["a2bd"], c["g2"],
        c["kkw"], c["ka"], c["rk"], c["seg"], c["tri"])
    y0, y1 = _scan(rt, kt, bt, at, be, ke, vr, pend, b, seq)
    x1 = _merge(y0, y1, bonus, g, ona, gates, x, c["seg"], c["lnw"], c["lnb"],
                c["wpa"], c["wpb"], c["wout"], c["npost_mix"])
    return _ffn(x1, c["npre_ffn"], c["wup"], c["wdn"], c["npost_ffn"])


def _trunk(x, layer_consts):
    b, seq, dm = x.shape
    assert dm == D_MODEL and seq % TOK_TILE == 0 and seq // GRID_W >= WIN_H
    h = x.reshape(b * seq, dm)
    for c in layer_consts:
        h = _encoder_layer(h, b, seq, c)
    return h.reshape(b, seq, dm)


def kernel(x_prompt, x_sample, w_in, b_gate, rpb, mu_prev, mu_next, w0, w2, a0, a2, g2, k_k, k_a, r_k, ln_w, ln_b, w_proj_a, w_proj_b, w_out, w_up, w_down, norm_pre_mix, norm_post_mix, norm_pre_ffn, norm_post_ffn):
    stacked = (w_in, b_gate, rpb, mu_prev, mu_next, w0, w2, a0, a2, g2, k_k, k_a, r_k,
               ln_w, ln_b, w_proj_a, w_proj_b, w_out, w_up, w_down,
               norm_pre_mix, norm_post_mix, norm_pre_ffn, norm_post_ffn)
    depth = w_in.shape[0]
    layer_consts = [_layer_consts(*[p[i] for p in stacked]) for i in range(depth)]
    return (_trunk(x_prompt, layer_consts), _trunk(x_sample, layer_consts))
```

```python
import functools
import itertools
import math

import jax
import jax.numpy as jnp
from jax import lax
from jax.experimental import pallas as pl
from jax.experimental.pallas import tpu as pltpu

F32 = jnp.float32
BF16 = jnp.bfloat16

D_MODEL = 1024
GRID_W = 64
WIN_H = 8
WIN_W = 16
NA_HEADS = 8
HEAD_DIM = 64
D_NA = NA_HEADS * HEAD_DIM
RWKV_HEADS = 8
D_RWKV = RWKV_HEADS * HEAD_DIM
DECAY_LORA = 64
ICLR_LORA = 64
GATE_LORA = 128
N_DIR = 2
D_FF = 4 * D_MODEL
D_SHIFT = 3 * D_RWKV + N_DIR * DECAY_LORA + N_DIR * ICLR_LORA + GATE_LORA
RMS_EPS = 1e-6
GN_EPS = 64e-5
KK_EPS = 1e-12
NEG_BIG = -1e30

CHUNK = 64
HALF = CHUNK // 2
assert CHUNK == HEAD_DIM
PAIR = 2 * HEAD_DIM
N_PAIR = D_RWKV // PAIR
TOK_TILE = 512
HALO = 8
ROWS_PER_TILE = TOK_TILE // GRID_W
CHUNKS_PER_TILE = TOK_TILE // CHUNK
NA_ROWS_PER_ITER = 2
V7X_VMEM_LIMIT = 56 * 1024 * 1024

_NT = (((1,), (1,)), ((), ()))


def _dot(a, b):
    return jnp.dot(a, b, preferred_element_type=F32)


def _sigmoid(x):
    return 0.5 * jnp.tanh(0.5 * x) + 0.5


def _params(sem):
    return pltpu.CompilerParams(dimension_semantics=sem, vmem_limit_bytes=V7X_VMEM_LIMIT)


def _const_spec(shape):
    nd = len(shape)
    return pl.BlockSpec(shape, lambda *_: (0,) * nd, pipeline_mode=pl.Buffered(1))


def _inproj_body(x_ref, xp_ref, xn_ref, g_ref, wqkv_ref, wsh_ref, wgt_ref, bg_ref,
                 mu0_ref, mup_ref, mun_ref, q_ref, k_ref, v_ref, sh_ref, gt_ref, *, tiles_per_seq):
    tm = TOK_TILE
    pos = lax.rem(pl.program_id(0), tiles_per_seq)
    xe = jnp.concatenate([xp_ref[...], x_ref[...], xn_ref[...]], axis=0)
    ms = jnp.mean(xe * xe, axis=-1, keepdims=True)
    he = xe * lax.rsqrt(ms + RMS_EPS) * g_ref[...]
    rowe = lax.broadcasted_iota(jnp.int32, (tm + 2 * HALO, 1), 0)
    inside = (((rowe >= HALO) | (pos != 0))
              & ((rowe < tm + HALO) | (pos != tiles_per_seq - 1)))
    he = jnp.where(inside, he, 0.0).astype(BF16)
    main = slice(HALO, tm + HALO)
    qkv = _dot(he, wqkv_ref[...])[main]
    q_ref[...] = (qkv[:, 0:D_NA] * (HEAD_DIM ** -0.5)).astype(BF16)
    k_ref[...] = qkv[:, D_NA:2 * D_NA].astype(BF16)
    v_ref[...] = qkv[:, 2 * D_NA:3 * D_NA].astype(BF16)
    pe = _dot(he, wsh_ref[...])
    sh_ref[...] = (mu0_ref[...] * pe[main] + mup_ref[...] * pe[HALO - 1:tm + HALO - 1]
                   + mun_ref[...] * pe[HALO + 1:tm + HALO + 1])
    gt_ref[...] = _sigmoid(_dot(he, wgt_ref[...])[main] + bg_ref[...]).astype(BF16)


def _inproj(x, seq, g, wqkv, wsh, wgt, bg, mu0, mup, mun):
    t = x.shape[0]
    tm = TOK_TILE
    nh = tm // HALO
    last = t // HALO - 1
    row = lambda w: pl.BlockSpec((tm, w), lambda i: (i, 0))
    consts = [g, wqkv, wsh, wgt, bg, mu0, mup, mun]
    return pl.pallas_call(
        functools.partial(_inproj_body, tiles_per_seq=seq // tm),
        grid=(t // tm,),
        in_specs=[row(D_MODEL),
                  pl.BlockSpec((HALO, D_MODEL), lambda i: (jnp.maximum(i * nh - 1, 0), 0)),
                  pl.BlockSpec((HALO, D_MODEL), lambda i: (jnp.minimum((i + 1) * nh, last), 0))]
        + [_const_spec(c.shape) for c in consts],
        out_specs=[row(D_NA), row(D_NA), row(D_NA), row(D_SHIFT), row(2 * D_MODEL)],
        out_shape=[jax.ShapeDtypeStruct((t, D_NA), BF16)] * 3
        + [jax.ShapeDtypeStruct((t, D_SHIFT), F32), jax.ShapeDtypeStruct((t, 2 * D_MODEL), BF16)],
        compiler_params=_params(("parallel",)),
        name="inproj",
    )(x, x, x, *consts)


def _na_bias_table(rpb):
    c = jnp.arange(GRID_W)
    col_start = jnp.clip(c - WIN_W // 2, 0, GRID_W - WIN_W)
    col_ok = (c[None, :] >= col_start[:, None]) & (c[None, :] < col_start[:, None] + WIN_W)
    col_idx = jnp.clip(c[None, :] - c[:, None] + (WIN_W - 1), 0, 2 * WIN_W - 2)
    onehot = (col_idx[:, :, None] == jnp.arange(2 * WIN_W - 1)[None, None, :]).astype(F32)
    cols = jnp.einsum("hdc,qkc->hqdk", rpb.astype(F32), onehot,
                      precision=lax.Precision.HIGHEST)
    cols = jnp.where(col_ok[None, :, None, :], cols, NEG_BIG)
    tab = jnp.stack([cols[:, :, WIN_H - 1 - e:2 * WIN_H - 1 - e, :] for e in range(WIN_H)])
    return tab.reshape(WIN_H, NA_HEADS, GRID_W, WIN_H * GRID_W)


def _natten_body(q_ref, k_ref, v_ref, bias_ref, o_ref, *, rows):
    j = pl.program_id(1)
    first_half = lax.broadcasted_iota(jnp.int32, (GRID_W, PAIR), 1) < HEAD_DIM
    n_keys = WIN_H * GRID_W
    zero_q = jnp.zeros((GRID_W, PAIR), BF16)

    def problems(row_ids):
        work = []
        for jr in row_ids:
            r = j * ROWS_PER_TILE + jr
            rs = jnp.clip(r - WIN_H // 2, 0, rows - WIN_H)
            off = pl.multiple_of(rs * GRID_W, GRID_W)
            for p in range(N_PAIR):
                work.append((r - rs, off, jr * GRID_W, p, slice(p * PAIR, (p + 1) * PAIR)))
        return work

    def qk(work):
        scores = []
        for e, off, q0, p, ls in work:
            qp = q_ref[q0:q0 + GRID_W, ls]
            q2 = jnp.concatenate([jnp.where(first_half, qp, zero_q),
                                  jnp.where(first_half, zero_q, qp)], axis=0)
            scores.append(lax.dot_general(q2, k_ref[pl.ds(off, n_keys), ls], _NT,
                                          preferred_element_type=F32))
        return scores

    def softmax(work, scores):
        probs, sums = [], []
        for (e, off, q0, p, ls), s2 in zip(work, scores):
            pr2 = []
            for hh in range(2):
                s = s2[hh * GRID_W:(hh + 1) * GRID_W] + bias_ref[e, 2 * p + hh]
                pr = jnp.exp(s - jnp.max(s, axis=-1, keepdims=True))
                sums.append(jnp.sum(pr, axis=-1, keepdims=True))
                pr2.append(pr.astype(BF16))
            probs.append(jnp.concatenate(pr2, axis=0))
        return probs, sums

    def pv(work, probs, sums):
        outs = [_dot(pr2, v_ref[pl.ds(off, n_keys), ls]) for (e, off, q0, p, ls), pr2 in zip(work, probs)]
        for i, ((e, off, q0, p, ls), o2) in enumerate(zip(work, outs)):
            o0 = o2[0:GRID_W] / sums[2 * i]
            o1 = o2[GRID_W:] / sums[2 * i + 1]
            o_ref[q0:q0 + GRID_W, ls] = jnp.where(first_half, o0, o1).astype(BF16)

    groups = [problems(range(g, g + NA_ROWS_PER_ITER)) for g in range(0, ROWS_PER_TILE, NA_ROWS_PER_ITER)]
    scores = qk(groups[0])
    for gi, work in enumerate(groups):
        nxt_scores = qk(groups[gi + 1]) if gi + 1 < len(groups) else None
        probs, sums = softmax(work, scores)
        pv(work, probs, sums)
        scores = nxt_scores


def _natten(q, k, v, bias, b, seq):
    t = q.shape[0]
    rows = seq // GRID_W
    nblk = seq // TOK_TILE
    blk = pl.BlockSpec((TOK_TILE, D_NA), lambda bi, j: (bi * nblk + j, 0))
    whole = pl.BlockSpec((seq, D_NA), lambda bi, j: (bi, 0))
    return pl.pallas_call(
        functools.partial(_natten_body, rows=rows),
        grid=(b, nblk),
        in_specs=[blk, whole, whole, _const_spec(bias.shape)],
        out_specs=blk,
        out_shape=jax.ShapeDtypeStruct((t, D_NA), BF16),
        compiler_params=_params(("parallel", "arbitrary")),
        name="natten",
    )(q, k, v, bias)


def _split2_dot(tri, x):
    x0 = x.astype(BF16)
    x1 = (x - x0.astype(F32)).astype(BF16)
    return _dot(tri, x0) + _dot(tri, x1)


def _prep_body(sh_ref, w0_ref, w2_ref, a0_ref, a2_ref, g2_ref,
               kkw_ref, ka_ref, rk_ref, seg_ref, tri_ref,
               rt_ref, kt_ref, bt_ref, at_ref, v_ref, pend_ref, g_ref, bonus_ref):
    sh = sh_ref[...]
    r = sh[:, 0:D_RWKV]
    k = sh[:, D_RWKV:2 * D_RWKV]
    v = sh[:, 2 * D_RWKV:3 * D_RWKV]
    off = 3 * D_RWKV
    wl = sh[:, off:off + 2 * DECAY_LORA]
    al = sh[:, off + 2 * DECAY_LORA:off + 2 * DECAY_LORA + 2 * ICLR_LORA]
    gl = sh[:, off + 2 * DECAY_LORA + 2 * ICLR_LORA:]

    zw = w0_ref[...] + _dot(jnp.tanh(wl).astype(BF16), w2_ref[...])
    za = a0_ref[...] + _dot(al.astype(BF16), a2_ref[...])
    g_ref[...] = _dot(_sigmoid(gl).astype(BF16), g2_ref[...]).astype(BF16)
    lw = (-math.exp(-0.5) * math.log2(math.e)) * _sigmoid(zw)
    iclr = _sigmoid(za)

    kkv = k * kkw_ref[...]
    kkn = kkv * lax.rsqrt(_dot((kkv * kkv).astype(BF16), seg_ref[...]) + KK_EPS)
    v_ref[...] = v.astype(BF16)

    bonus_acc = jnp.zeros_like(r)
    for d in range(N_DIR):
        ds_ = slice(d * D_RWKV, (d + 1) * D_RWKV)
        icl = iclr[:, ds_]
        kd = k * (1.0 + (icl - 1.0) * ka_ref[...])
        bd = kkn * icl
        bonus_acc = bonus_acc + r * kd * rk_ref[...]
        lwd = lw[:, ds_]
        tri = tri_ref[d]
        for c in range(CHUNKS_PER_TILE):
            cs = slice(c * CHUNK, (c + 1) * CHUNK)
            x = lwd[cs]
            lp = _split2_dot(tri, x)
            lend = lp[CHUNK - 1:CHUNK] if d == 0 else lp[0:1]
            pw = jnp.exp2(lp)
            pinv = jnp.exp2(-lp)
            pex = jnp.exp2(lp - x)
            rt_ref[d, cs, :] = (r[cs] * pw).astype(BF16)
            kt_ref[d, cs, :] = (kd[cs] * pinv).astype(BF16)
            bt_ref[d, cs, :] = (bd[cs] * pinv).astype(BF16)
            at_ref[d, cs, :] = (-kkn[cs] * pex).astype(BF16)
            pend_ref[d, c] = jnp.exp2(lend)
    bonus_ref[...] = (_dot(bonus_acc.astype(BF16), seg_ref[...]) * v).astype(BF16)


def _prep(sh, w0, w2bd, a0, a2bd, g2, kkw, ka, rk, seg, tri):
    t = sh.shape[0]
    tb = TOK_TILE
    row = lambda w: pl.BlockSpec((tb, w), lambda i: (i, 0))
    drow = pl.BlockSpec((N_DIR, tb, D_RWKV), lambda i: (0, i, 0))
    consts = [w0, w2bd, a0, a2bd, g2, kkw, ka, rk, seg, tri]
    bf = jax.ShapeDtypeStruct((N_DIR, t, D_RWKV), BF16)
    return pl.pallas_call(
        _prep_body,
        grid=(t // tb,),
        in_specs=[row(D_SHIFT)] + [_const_spec(c.shape) for c in consts],
        out_specs=[drow] * 4 + [row(D_RWKV),
                                pl.BlockSpec((N_DIR, CHUNKS_PER_TILE, 1, D_RWKV), lambda i: (0, i, 0, 0)),
                                row(D_RWKV), row(D_RWKV)],
        out_shape=[bf] * 4 + [jax.ShapeDtypeStruct((t, D_RWKV), BF16),
                              jax.ShapeDtypeStruct((N_DIR, t // CHUNK, 1, D_RWKV), F32),
                              jax.ShapeDtypeStruct((t, D_RWKV), BF16),
                              jax.ShapeDtypeStruct((t, D_RWKV), BF16)],
        compiler_params=_params(("parallel",)),
        name="rwkv_prep",
    )(sh, *consts)


def _scan_factors(d, rt_ref, kt_ref, bt_ref, at_ref, v_ref, pend_ref, y0_scr, qm_scr, d_scr):
    sign = 1 - 2 * d
    rowi2 = lax.broadcasted_iota(jnp.int32, (CHUNK, 2 * PAIR), 0)
    lane2 = lax.broadcasted_iota(jnp.int32, (CHUNK, 2 * PAIR), 1)
    diff = (rowi2 - (lane2 & (HEAD_DIM - 1))) * sign
    strict2 = diff > 0
    incl2 = diff >= 0
    low2 = (lane2 & HEAD_DIM) == 0
    first_half = lax.broadcasted_iota(jnp.int32, (CHUNK, PAIR), 1) < HEAD_DIM
    rowp = lax.broadcasted_iota(jnp.int32, (PAIR, PAIR), 0)
    lanep = lax.broadcasted_iota(jnp.int32, (PAIR, PAIR), 1)
    same_half = (rowp < HEAD_DIM) == (lanep < HEAD_DIM)
    eye_pair = rowp == lanep
    rowh = lax.broadcasted_iota(jnp.int32, (HALF, PAIR), 0)
    blk = lax.broadcasted_iota(jnp.int32, (HALF, PAIR), 1) // HALF
    first_blk = (blk & 1) == d
    eye_pk = ((lax.broadcasted_iota(jnp.int32, (HALF, PAIR), 1) & (HALF - 1)) == rowh).astype(F32)
    hi_rows = slice((1 - d) * HALF, (2 - d) * HALF)
    zero = jnp.zeros((CHUNK, PAIR), BF16)
    zero2 = jnp.zeros((CHUNK, 2 * PAIR), BF16)
    zero_h = jnp.zeros((HALF, PAIR), BF16)
    cat0 = lambda xs: jnp.concatenate(xs, axis=0)
    cat1 = lambda xs: jnp.concatenate(xs, axis=1)
    sel = lambda a, hh: jnp.where(first_half, a, zero) if hh == 0 else jnp.where(first_half, zero, a)
    lo2 = lambda a: jnp.where(low2, a, zero2)
    hi2 = lambda a: jnp.where(low2, zero2, a)
    only = lambda a, i: jnp.where(blk == i, a, zero_h)
    block_diag4 = lambda a: cat0([only(a, i) for i in range(4)])

    prs = []
    for c in range(CHUNKS_PER_TILE):
        ts = slice(c * CHUNK, (c + 1) * CHUNK)
        for p in range(N_PAIR):
            ls = slice(p * PAIR, (p + 1) * PAIR)
            ld = lambda ref: ref[0, ts, ls]
            prs.append(dict(c=c, ts=ts, p=p, ls=ls, rt=ld(rt_ref), kt=ld(kt_ref), bt=ld(bt_ref),
                            at=ld(at_ref), v=v_ref[ts, ls], pend=pend_ref[0, c, :, ls]))
    for pr in prs:
        rhs = cat0([sel(pr["bt"], 0), sel(pr["kt"], 0), sel(pr["kt"], 1), sel(pr["bt"], 1)])
        aa = lax.dot_general(cat0([pr["at"], pr["rt"]]), rhs, _NT, preferred_element_type=F32)
        top = jnp.where(strict2, aa[0:CHUNK], 0.0)
        pr["top"] = top.astype(BF16)
        pr["bot"] = jnp.where(incl2, aa[CHUNK:], 0.0).astype(BF16)
        t0_, t1_ = top[:, :PAIR], top[:, PAIR:]
        pr["ppk"] = jnp.where(blk == 0, t0_[0:HALF], jnp.where(blk == 1, t0_[HALF:],
                              jnp.where(blk == 2, t1_[0:HALF], t1_[HALF:])))
        pr["ahl"] = jnp.where(blk == d, t0_[hi_rows], jnp.where(blk == 2 + d, t1_[hi_rows], 0.0)).astype(BF16)
        pr["spk"] = eye_pk
    yield
    for pr in prs:
        v, top = pr["v"], pr["top"]
        pr["akv"] = (_dot(top[:, :PAIR], cat0([zero, v])).astype(BF16),
                     _dot(top[:, PAIR:], cat0([v, zero])).astype(BF16))
    yield
    for _ in range(int(math.log2(HALF))):
        res = []
        for pr in prs:
            pb = pr["ppk"].astype(BF16)
            res.append(_dot(pb, cat1([block_diag4(pb), block_diag4(pr["spk"].astype(BF16))])))
        for pr, r in zip(prs, res):
            pr["ppk"] = r[:, :PAIR]
            pr["spk"] = pr["spk"] + r[:, PAIR:]
        yield
    res = []
    for pr in prs:
        sb = pr["spk"].astype(BF16)
        pr["sb"] = sb
        res.append(_dot(pr["ahl"], cat0([jnp.where(first_blk & (blk == i), sb, zero_h) for i in range(4)])))
    yield
    xs = [r.astype(BF16) for r in res]
    res = []
    for pr, xb in zip(prs, xs):
        rows = [zero_h] * 4
        rows[1 - d] = only(xb, d)
        rows[3 - d] = only(xb, 2 + d)
        res.append(_dot(jnp.where(first_blk, zero_h, pr["sb"]), cat0(rows)))
    yield
    for pr, thl in zip(prs, res):
        sb = pr["sb"]
        t_lo = jnp.where(first_blk, sb, zero_h)
        t_hi = jnp.where(first_blk, thl.astype(BF16), sb)
        tinv = cat0([t_lo, t_hi]) if d == 0 else cat0([t_hi, t_lo])
        w0 = cat1([sel(pr["at"], 0), sel(pr["akv"][0], 0)])
        w1 = cat1([sel(pr["at"], 1), sel(pr["akv"][1], 1)])
        pr["au"] = _dot(tinv, cat0([w0, w1])).astype(BF16)
        pr["zv"] = cat1([zero, pr["v"]])
    yield
    for pr in prs:
        au, zv = pr["au"], pr["zv"]
        pr["o"] = _dot(pr["bot"], cat0([lo2(au), lo2(zv), hi2(zv), hi2(au)]))
        bek = cat0([pr["bt"].astype(F32) * pr["pend"], pr["kt"].astype(F32) * pr["pend"]]).astype(BF16)
        pr["md"] = _dot(bek.T, cat0([au, zv]))
    yield
    for pr in prs:
        o, md, ts, ls = pr["o"], pr["md"], pr["ts"], pr["ls"]
        y0_scr[ts, ls] = o[:, PAIR:]
        mbd = jnp.where(eye_pair, pr["pend"], 0.0) + jnp.where(same_half, md[:, :PAIR], 0.0)
        qm_scr[pr["c"], pr["p"]] = cat0([(pr["rt"].astype(F32) + o[:, :PAIR]).astype(BF16), mbd.astype(BF16)])
        d_scr[pr["c"], pr["p"]] = jnp.where(same_half, md[:, PAIR:], 0.0)


def _scan_body(*refs):
    n_in = 6
    ins = (refs[0:n_in], refs[n_in:2 * n_in])
    ys = refs[2 * n_in:2 * n_in + 2]
    h_scr, y0_scr, qm_scr, d_scr = refs[2 * n_in + 2:]

    @pl.when(pl.program_id(1) == 0)
    def _():
        h_scr[...] = jnp.zeros_like(h_scr)

    stages = [_scan_factors(d, *ins[d], y0_scr.at[d], qm_scr.at[d], d_scr.at[d]) for d in range(N_DIR)]
    for _ in itertools.zip_longest(*stages):
        pass

    states = [h_scr[d, p] for d in range(N_DIR) for p in range(N_PAIR)]
    for ci in range(CHUNKS_PER_TILE):
        work = []
        for d in range(N_DIR):
            c = ci if d == 0 else CHUNKS_PER_TILE - 1 - ci
            for p in range(N_PAIR):
                work.append((d, c, slice(c * CHUNK, (c + 1) * CHUNK), p, slice(p * PAIR, (p + 1) * PAIR)))
        res = [_dot(qm_scr[d, c, p], h.astype(BF16)) for (d, c, ts, p, ls), h in zip(work, states)]
        states = [r[CHUNK:] + d_scr[d, c, p] for (d, c, ts, p, ls), r in zip(work, res)]
        for (d, c, ts, p, ls), r in zip(work, res):
            ys[d][ts, ls] = (y0_scr[d, ts, ls] + r[0:CHUNK]).astype(BF16)
    for i, h in enumerate(states):
        h_scr[i // N_PAIR, i % N_PAIR] = h


def _scan(rt, kt, bt, at, v, pend, b, seq):
    t = v.shape[0]
    tc = TOK_TILE
    nblk = seq // tc
    tok = (lambda bi, j: bi * nblk + j, lambda bi, j: bi * nblk + nblk - 1 - j)
    in_specs, args = [], []
    for d in range(N_DIR):
        dblk = pl.BlockSpec((1, tc, D_RWKV), lambda bi, j, d=d: (d, tok[d](bi, j), 0))
        in_specs += [dblk] * 4 + [
            pl.BlockSpec((tc, D_RWKV), lambda bi, j, d=d: (tok[d](bi, j), 0)),
            pl.BlockSpec((1, CHUNKS_PER_TILE, 1, D_RWKV), lambda bi, j, d=d: (d, tok[d](bi, j), 0, 0))]
        args += [rt, kt, bt, at, v, pend]
    return pl.pallas_call(
        _scan_body,
        grid=(b, nblk),
        in_specs=in_specs,
        out_specs=[pl.BlockSpec((tc, D_RWKV), lambda bi, j, d=d: (tok[d](bi, j), 0)) for d in range(N_DIR)],
        out_shape=[jax.ShapeDtypeStruct((t, D_RWKV), BF16)] * N_DIR,
        scratch_shapes=[pltpu.VMEM((N_DIR, N_PAIR, PAIR, PAIR), F32),
                        pltpu.VMEM((N_DIR, tc, D_RWKV), F32),
                        pltpu.VMEM((N_DIR, CHUNKS_PER_TILE, N_PAIR, CHUNK + PAIR, PAIR), BF16),
                        pltpu.VMEM((N_DIR, CHUNKS_PER_TILE, N_PAIR, PAIR, PAIR), F32)],
        compiler_params=_params(("parallel", "arbitrary")),
        name="rwkv_scan",
    )(*args)


def _merge_body(y0_ref, y1_ref, bonus_ref, g_ref, ona_ref, gt_ref, x_ref, seg_ref, lnw_ref, lnb_ref,
                wpa_ref, wpb_ref, wout_ref, npost_ref, o_ref):
    inv_n = 1.0 / HEAD_DIM
    y = y0_ref[...].astype(F32) + y1_ref[...]
    mean = _dot(y.astype(BF16), seg_ref[...]) * inv_n
    yc = y - mean
    var = _dot((yc * yc).astype(BF16), seg_ref[...]) * inv_n
    yn = yc * lax.rsqrt(var + GN_EPS) * lnw_ref[...] + lnb_ref[...]
    orw = ((yn + bonus_ref[...]) * g_ref[...]).astype(BF16)
    gates = gt_ref[...]
    merged = (gates[:, :D_MODEL] * _dot(ona_ref[...], wpa_ref[...])
              + gates[:, D_MODEL:] * _dot(orw, wpb_ref[...]))
    z = _dot(merged.astype(BF16), wout_ref[...])
    ms = jnp.mean(z * z, axis=-1, keepdims=True)
    o_ref[...] = x_ref[...] + z * lax.rsqrt(ms + RMS_EPS) * npost_ref[...]


def _merge(y0, y1, bonus, g, ona, gates, x, seg, lnw, lnb, wpa, wpb, wout, npost):
    t = x.shape[0]
    tm = TOK_TILE
    row = lambda w: pl.BlockSpec((tm, w), lambda i: (i, 0))
    consts = [seg, lnw, lnb, wpa, wpb, wout, npost]
    return pl.pallas_call(
        _merge_body,
        grid=(t // tm,),
        in_specs=[row(D_RWKV), row(D_RWKV), row(D_RWKV), row(D_RWKV), row(D_NA), row(2 * D_MODEL), row(D_MODEL)]
        + [_const_spec(c.shape) for c in consts],
        out_specs=row(D_MODEL),
        out_shape=jax.ShapeDtypeStruct((t, D_MODEL), F32),
        compiler_params=_params(("parallel",)),
        name="merge",
    )(y0, y1, bonus, g, ona, gates, x, *consts)


def _ffn_body(x_ref, npre_ref, wup_ref, wdn_ref, npost_ref, o_ref):
    x = x_ref[...]
    ms = jnp.mean(x * x, axis=-1, keepdims=True)
    h = (x * lax.rsqrt(ms + RMS_EPS) * npre_ref[...]).astype(BF16)
    u = jnp.maximum(_dot(h, wup_ref[...]), 0.0)
    f = _dot((u * u).astype(BF16), wdn_ref[...])
    ms = jnp.mean(f * f, axis=-1, keepdims=True)
    o_ref[...] = x + f * lax.rsqrt(ms + RMS_EPS) * npost_ref[...]


def _ffn(x, npre, wup, wdn, npost):
    t = x.shape[0]
    tm = TOK_TILE
    row = pl.BlockSpec((tm, D_MODEL), lambda i: (i, 0))
    consts = [npre, wup, wdn, npost]
    return pl.pallas_call(
        _ffn_body,
        grid=(t // tm,),
        in_specs=[row] + [_const_spec(c.shape) for c in consts],
        out_specs=row,
        out_shape=jax.ShapeDtypeStruct((t, D_MODEL), F32),
        compiler_params=_params(("parallel",)),
        name="ffn",
    )(x, *consts)


def _block_diag2(w):
    z = jnp.zeros_like(w[0])
    return jnp.concatenate([jnp.concatenate([w[0], z], axis=1),
                            jnp.concatenate([z, w[1]], axis=1)], axis=0)


def _layer_consts(w_in, b_gate, rpb, mu_prev, mu_next, w0, w2, a0, a2, g2, k_k, k_a, r_k,
                  ln_w, ln_b, w_proj_a, w_proj_b, w_out, w_up, w_down,
                  norm_pre_mix, norm_post_mix, norm_pre_ffn, norm_post_ffn):
    rowv = lambda a: a.reshape(1, -1).astype(F32)
    head = jnp.arange(D_RWKV) // HEAD_DIM
    idx = jnp.arange(CHUNK)
    tril = (idx[:, None] >= idx[None, :])
    return dict(
        wqkv=w_in[:, :3 * D_NA].astype(BF16),
        wsh=w_in[:, 3 * D_NA:3 * D_NA + D_SHIFT].astype(BF16),
        wgt=w_in[:, 3 * D_NA + D_SHIFT:].astype(BF16),
        bg=rowv(b_gate), bias=_na_bias_table(rpb),
        mu0=rowv(1.0 - mu_prev - mu_next), mup=rowv(mu_prev), mun=rowv(mu_next),
        w0=rowv(w0), w2bd=_block_diag2(w2).astype(BF16),
        a0=rowv(a0), a2bd=_block_diag2(a2).astype(BF16),
        g2=g2.astype(BF16), kkw=rowv(k_k), ka=rowv(k_a), rk=rowv(r_k),
        seg=(head[:, None] == head[None, :]).astype(BF16),
        tri=jnp.stack([tril, tril.T]).astype(BF16),
        lnw=rowv(ln_w), lnb=rowv(ln_b),
        wpa=w_proj_a.astype(BF16), wpb=w_proj_b.astype(BF16), wout=w_out.astype(BF16),
        wup=w_up.astype(BF16), wdn=w_down.astype(BF16),
        npre_mix=rowv(norm_pre_mix), npost_mix=rowv(norm_post_mix),
        npre_ffn=rowv(norm_pre_ffn), npost_ffn=rowv(norm_post_ffn),
    )


def _encoder_layer(x, b, seq, c):
    q, k, v, sh, gates = _inproj(x, seq, c["npre_mix"], c["wqkv"], c["wsh"], c["wgt"], c["bg"],
                                 c["mu0"], c["mup"], c["mun"])
    ona = _natten(q, k, v, c["bias"], b, seq)
    rt, kt, bt, at, vr, pend, g, bonus = _prep(
        sh, c["w0"], c["w2bd"], c["a0"], c["a2bd"], c["g2"],
        c["kkw"], c["ka"], c["rk"], c["seg"], c["tri"])
    y0, y1 = _scan(rt, kt, bt, at, vr, pend, b, seq)
    x1 = _merge(y0, y1, bonus, g, ona, gates, x, c["seg"], c["lnw"], c["lnb"],
                c["wpa"], c["wpb"], c["wout"], c["npost_mix"])
    return _ffn(x1, c["npre_ffn"], c["wup"], c["wdn"], c["npost_ffn"])


def _trunk(x, layer_consts):
    b, seq, dm = x.shape
    assert dm == D_MODEL and seq % TOK_TILE == 0 and seq // GRID_W >= WIN_H
    h = x.reshape(b * seq, dm)
    for c in layer_consts:
        h = _encoder_layer(h, b, seq, c)
    return h.reshape(b, seq, dm)


def kernel(x_prompt, x_sample, w_in, b_gate, rpb, mu_prev, mu_next, w0, w2, a0, a2, g2, k_k, k_a, r_k, ln_w, ln_b, w_proj_a, w_proj_b, w_out, w_up, w_down, norm_pre_mix, norm_post_mix, norm_pre_ffn, norm_post_ffn):
    stacked = (w_in, b_gate, rpb, mu_prev, mu_next, w0, w2, a0, a2, g2, k_k, k_a, r_k,
               ln_w, ln_b, w_proj_a, w_proj_b, w_out, w_up, w_down,
               norm_pre_mix, norm_post_mix, norm_pre_ffn, norm_post_ffn)
    depth = w_in.shape[0]
    layer_consts = [_layer_consts(*[p[i] for p in stacked]) for i in range(depth)]
    return (_trunk(x_prompt, layer_consts), _trunk(x_sample, layer_consts))
```

```python
import functools
import itertools
import math

import jax
import jax.numpy as jnp
from jax import lax
from jax.experimental import pallas as pl
from jax.experimental.pallas import tpu as pltpu

F32 = jnp.float32
BF16 = jnp.bfloat16

D_MODEL = 1024
GRID_W = 64
WIN_H = 8
WIN_W = 16
NA_HEADS = 8
HEAD_DIM = 64
D_NA = NA_HEADS * HEAD_DIM
RWKV_HEADS = 8
D_RWKV = RWKV_HEADS * HEAD_DIM
DECAY_LORA = 64
ICLR_LORA = 64
GATE_LORA = 128
N_DIR = 2
D_FF = 4 * D_MODEL
D_SHIFT = 3 * D_RWKV + N_DIR * DECAY_LORA + N_DIR * ICLR_LORA + GATE_LORA
RMS_EPS = 1e-6
GN_EPS = 64e-5
KK_EPS = 1e-12
NEG_BIG = -1e30

CHUNK = 64
HALF = CHUNK // 2
assert CHUNK == HEAD_DIM
PAIR = 2 * HEAD_DIM
N_PAIR = D_RWKV // PAIR
TOK_TILE = 512
HALO = 8
ROWS_PER_TILE = TOK_TILE // GRID_W
CHUNKS_PER_TILE = TOK_TILE // CHUNK
NA_ROWS_PER_ITER = 1
V7X_VMEM_LIMIT = 56 * 1024 * 1024

_NT = (((1,), (1,)), ((), ()))


def _dot(a, b):
    return jnp.dot(a, b, preferred_element_type=F32)


def _sigmoid(x):
    return 0.5 * jnp.tanh(0.5 * x) + 0.5


def _params(sem):
    return pltpu.CompilerParams(dimension_semantics=sem, vmem_limit_bytes=V7X_VMEM_LIMIT)


def _const_spec(shape):
    nd = len(shape)
    return pl.BlockSpec(shape, lambda *_: (0,) * nd, pipeline_mode=pl.Buffered(1))


def _inproj_body(x_ref, xp_ref, xn_ref, g_ref, wmain_ref, wsh_ref, bg_ref,
                 mu0_ref, mup_ref, mun_ref, q_ref, k_ref, v_ref, sh_ref, gt_ref, *, tiles_per_seq):
    tm = TOK_TILE
    pos = lax.rem(pl.program_id(0), tiles_per_seq)
    xe = jnp.concatenate([xp_ref[...], x_ref[...], xn_ref[...]], axis=0)
    ms = jnp.mean(xe * xe, axis=-1, keepdims=True)
    he = xe * lax.rsqrt(ms + RMS_EPS) * g_ref[...]
    rowe = lax.broadcasted_iota(jnp.int32, (tm + 2 * HALO, 1), 0)
    inside = (((rowe >= HALO) | (pos != 0))
              & ((rowe < tm + HALO) | (pos != tiles_per_seq - 1)))
    he = jnp.where(inside, he, 0.0)
    main = slice(HALO, tm + HALO)
    h = he[main].astype(BF16)
    qkv_gt = _dot(h, wmain_ref[...])
    qkv = qkv_gt[:, 0:3 * D_NA]
    q_ref[...] = (qkv[:, 0:D_NA] * (HEAD_DIM ** -0.5)).astype(BF16)
    k_ref[...] = qkv[:, D_NA:2 * D_NA].astype(BF16)
    v_ref[...] = qkv[:, 2 * D_NA:3 * D_NA].astype(BF16)
    pe = _dot(he.astype(BF16), wsh_ref[...])
    sh_ref[...] = (mu0_ref[...] * pe[main] + mup_ref[...] * pe[HALO - 1:tm + HALO - 1]
                   + mun_ref[...] * pe[HALO + 1:tm + HALO + 1])
    gt_ref[...] = _sigmoid(qkv_gt[:, 3 * D_NA:] + bg_ref[...]).astype(BF16)


def _inproj(x, seq, g, wmain, wsh, bg, mu0, mup, mun):
    t = x.shape[0]
    tm = TOK_TILE
    nh = tm // HALO
    last = t // HALO - 1
    row = lambda w: pl.BlockSpec((tm, w), lambda i: (i, 0))
    consts = [g, wmain, wsh, bg, mu0, mup, mun]
    return pl.pallas_call(
        functools.partial(_inproj_body, tiles_per_seq=seq // tm),
        grid=(t // tm,),
        in_specs=[row(D_MODEL),
                  pl.BlockSpec((HALO, D_MODEL), lambda i: (jnp.maximum(i * nh - 1, 0), 0)),
                  pl.BlockSpec((HALO, D_MODEL), lambda i: (jnp.minimum((i + 1) * nh, last), 0))]
        + [_const_spec(c.shape) for c in consts],
        out_specs=[row(D_NA), row(D_NA), row(D_NA), row(D_SHIFT), row(2 * D_MODEL)],
        out_shape=[jax.ShapeDtypeStruct((t, D_NA), BF16)] * 3
        + [jax.ShapeDtypeStruct((t, D_SHIFT), F32), jax.ShapeDtypeStruct((t, 2 * D_MODEL), BF16)],
        compiler_params=_params(("parallel",)),
        name="inproj",
    )(x, x, x, *consts)


def _na_bias_table(rpb):
    c = jnp.arange(GRID_W)
    col_start = jnp.clip(c - WIN_W // 2, 0, GRID_W - WIN_W)
    col_ok = (c[None, :] >= col_start[:, None]) & (c[None, :] < col_start[:, None] + WIN_W)
    col_idx = jnp.clip(c[None, :] - c[:, None] + (WIN_W - 1), 0, 2 * WIN_W - 2)
    onehot = (col_idx[:, :, None] == jnp.arange(2 * WIN_W - 1)[None, None, :]).astype(F32)
    cols = jnp.einsum("hdc,qkc->hqdk", rpb.astype(F32), onehot,
                      precision=lax.Precision.HIGHEST)
    cols = jnp.where(col_ok[None, :, None, :], cols, NEG_BIG)
    tab = jnp.stack([cols[:, :, WIN_H - 1 - e:2 * WIN_H - 1 - e, :] for e in range(WIN_H)])
    return tab.reshape(WIN_H, NA_HEADS, GRID_W, WIN_H * GRID_W)


def _natten_body(q_ref, k_ref, v_ref, bias_ref, o_ref, *, rows):
    j = pl.program_id(1)
    first_half = lax.broadcasted_iota(jnp.int32, (GRID_W, PAIR), 1) < HEAD_DIM
    n_keys = WIN_H * GRID_W
    zero_q = jnp.zeros((GRID_W, PAIR), BF16)

    def problems(row_ids):
        work = []
        for jr in row_ids:
            r = j * ROWS_PER_TILE + jr
            rs = jnp.clip(r - WIN_H // 2, 0, rows - WIN_H)
            off = pl.multiple_of(rs * GRID_W, GRID_W)
            for p in range(N_PAIR):
                work.append((r - rs, off, jr * GRID_W, p, slice(p * PAIR, (p + 1) * PAIR)))
        return work

    def qk(work):
        scores = []
        for e, off, q0, p, ls in work:
            qp = q_ref[q0:q0 + GRID_W, ls]
            q2 = jnp.concatenate([jnp.where(first_half, qp, zero_q),
                                  jnp.where(first_half, zero_q, qp)], axis=0)
            scores.append(lax.dot_general(q2, k_ref[pl.ds(off, n_keys), ls], _NT,
                                          preferred_element_type=F32))
        return scores

    def softmax(work, scores):
        probs, sums = [], []
        for (e, off, q0, p, ls), s2 in zip(work, scores):
            pr2 = []
            for hh in range(2):
                s = s2[hh * GRID_W:(hh + 1) * GRID_W] + bias_ref[e, 2 * p + hh]
                pr = jnp.exp(s - jnp.max(s, axis=-1, keepdims=True))
                sums.append(jnp.sum(pr, axis=-1, keepdims=True))
                pr2.append(pr.astype(BF16))
            probs.append(jnp.concatenate(pr2, axis=0))
        return probs, sums

    def pv(work, probs, sums):
        outs = [_dot(pr2, v_ref[pl.ds(off, n_keys), ls]) for (e, off, q0, p, ls), pr2 in zip(work, probs)]
        for i, ((e, off, q0, p, ls), o2) in enumerate(zip(work, outs)):
            o0 = o2[0:GRID_W] / sums[2 * i]
            o1 = o2[GRID_W:] / sums[2 * i + 1]
            o_ref[q0:q0 + GRID_W, ls] = jnp.where(first_half, o0, o1).astype(BF16)

    groups = [problems(range(g, g + NA_ROWS_PER_ITER)) for g in range(0, ROWS_PER_TILE, NA_ROWS_PER_ITER)]
    scores = qk(groups[0])
    for gi, work in enumerate(groups):
        nxt_scores = qk(groups[gi + 1]) if gi + 1 < len(groups) else None
        probs, sums = softmax(work, scores)
        pv(work, probs, sums)
        scores = nxt_scores


def _natten(q, k, v, bias, b, seq):
    t = q.shape[0]
    rows = seq // GRID_W
    nblk = seq // TOK_TILE
    blk = pl.BlockSpec((TOK_TILE, D_NA), lambda bi, j: (bi * nblk + j, 0))
    whole = pl.BlockSpec((seq, D_NA), lambda bi, j: (bi, 0))
    return pl.pallas_call(
        functools.partial(_natten_body, rows=rows),
        grid=(b, nblk),
        in_specs=[blk, whole, whole, _const_spec(bias.shape)],
        out_specs=blk,
        out_shape=jax.ShapeDtypeStruct((t, D_NA), BF16),
        compiler_params=_params(("parallel", "arbitrary")),
        name="natten",
    )(q, k, v, bias)


def _split2_dot(tri, x):
    x0 = x.astype(BF16)
    x1 = (x - x0.astype(F32)).astype(BF16)
    return _dot(tri, x0) + _dot(tri, x1)


def _prep_body(sh_ref, w0_ref, w2_ref, a0_ref, a2_ref, g2_ref,
               kkw_ref, ka_ref, rk_ref, seg_ref, tri_ref,
               rt_ref, kt_ref, bt_ref, at_ref, v_ref, pend_ref, g_ref, bonus_ref):
    sh = sh_ref[...]
    r = sh[:, 0:D_RWKV]
    k = sh[:, D_RWKV:2 * D_RWKV]
    v = sh[:, 2 * D_RWKV:3 * D_RWKV]
    off = 3 * D_RWKV
    wl = sh[:, off:off + 2 * DECAY_LORA]
    al = sh[:, off + 2 * DECAY_LORA:off + 2 * DECAY_LORA + 2 * ICLR_LORA]
    gl = sh[:, off + 2 * DECAY_LORA + 2 * ICLR_LORA:]

    zw = w0_ref[...] + _dot(jnp.tanh(wl).astype(BF16), w2_ref[...])
    za = a0_ref[...] + _dot(al.astype(BF16), a2_ref[...])
    g_ref[...] = _dot(_sigmoid(gl).astype(BF16), g2_ref[...]).astype(BF16)
    lw = (-math.exp(-0.5) * math.log2(math.e)) * _sigmoid(zw)
    iclr = _sigmoid(za)

    kkv = k * kkw_ref[...]
    kkn = kkv * lax.rsqrt(_dot((kkv * kkv).astype(BF16), seg_ref[...]) + KK_EPS)
    v_ref[...] = v.astype(BF16)

    bonus_acc = jnp.zeros_like(r)
    for d in range(N_DIR):
        ds_ = slice(d * D_RWKV, (d + 1) * D_RWKV)
        icl = iclr[:, ds_]
        kd = k * (1.0 + (icl - 1.0) * ka_ref[...])
        bd = kkn * icl
        bonus_acc = bonus_acc + r * kd * rk_ref[...]
        lwd = lw[:, ds_]
        tri = tri_ref[d]
        for c in range(CHUNKS_PER_TILE):
            cs = slice(c * CHUNK, (c + 1) * CHUNK)
            x = lwd[cs]
            lp = _split2_dot(tri, x)
            lend = lp[CHUNK - 1:CHUNK] if d == 0 else lp[0:1]
            pw = jnp.exp2(lp)
            pinv = jnp.exp2(-lp)
            pex = jnp.exp2(lp - x)
            rt_ref[d, cs, :] = (r[cs] * pw).astype(BF16)
            kt_ref[d, cs, :] = (kd[cs] * pinv).astype(BF16)
            bt_ref[d, cs, :] = (bd[cs] * pinv).astype(BF16)
            at_ref[d, cs, :] = (-kkn[cs] * pex).astype(BF16)
            pend_ref[d, c] = jnp.exp2(lend)
    bonus_ref[...] = (_dot(bonus_acc.astype(BF16), seg_ref[...]) * v).astype(BF16)


def _prep(sh, w0, w2bd, a0, a2bd, g2, kkw, ka, rk, seg, tri):
    t = sh.shape[0]
    tb = TOK_TILE
    row = lambda w: pl.BlockSpec((tb, w), lambda i: (i, 0))
    drow = pl.BlockSpec((N_DIR, tb, D_RWKV), lambda i: (0, i, 0))
    consts = [w0, w2bd, a0, a2bd, g2, kkw, ka, rk, seg, tri]
    bf = jax.ShapeDtypeStruct((N_DIR, t, D_RWKV), BF16)
    return pl.pallas_call(
        _prep_body,
        grid=(t // tb,),
        in_specs=[row(D_SHIFT)] + [_const_spec(c.shape) for c in consts],
        out_specs=[drow] * 4 + [row(D_RWKV),
                                pl.BlockSpec((N_DIR, CHUNKS_PER_TILE, 1, D_RWKV), lambda i: (0, i, 0, 0)),
                                row(D_RWKV), row(D_RWKV)],
        out_shape=[bf] * 4 + [jax.ShapeDtypeStruct((t, D_RWKV), BF16),
                              jax.ShapeDtypeStruct((N_DIR, t // CHUNK, 1, D_RWKV), F32),
                              jax.ShapeDtypeStruct((t, D_RWKV), BF16),
                              jax.ShapeDtypeStruct((t, D_RWKV), BF16)],
        compiler_params=_params(("parallel",)),
        name="rwkv_prep",
    )(sh, *consts)


def _scan_factors(d, rt_ref, kt_ref, bt_ref, at_ref, v_ref, pend_ref, y0_scr, qm_scr, d_scr):
    sign = 1 - 2 * d
    rowi2 = lax.broadcasted_iota(jnp.int32, (CHUNK, 2 * PAIR), 0)
    lane2 = lax.broadcasted_iota(jnp.int32, (CHUNK, 2 * PAIR), 1)
    diff = (rowi2 - (lane2 & (HEAD_DIM - 1))) * sign
    strict2 = diff > 0
    incl2 = diff >= 0
    low2 = (lane2 & HEAD_DIM) == 0
    first_half = lax.broadcasted_iota(jnp.int32, (CHUNK, PAIR), 1) < HEAD_DIM
    rowp = lax.broadcasted_iota(jnp.int32, (PAIR, PAIR), 0)
    lanep = lax.broadcasted_iota(jnp.int32, (PAIR, PAIR), 1)
    same_half = (rowp < HEAD_DIM) == (lanep < HEAD_DIM)
    eye_pair = rowp == lanep
    rowh = lax.broadcasted_iota(jnp.int32, (HALF, PAIR), 0)
    blk = lax.broadcasted_iota(jnp.int32, (HALF, PAIR), 1) // HALF
    first_blk = (blk & 1) == d
    eye_pk = ((lax.broadcasted_iota(jnp.int32, (HALF, PAIR), 1) & (HALF - 1)) == rowh).astype(F32)
    hi_rows = slice((1 - d) * HALF, (2 - d) * HALF)
    zero = jnp.zeros((CHUNK, PAIR), BF16)
    zero2 = jnp.zeros((CHUNK, 2 * PAIR), BF16)
    zero_h = jnp.zeros((HALF, PAIR), BF16)
    cat0 = lambda xs: jnp.concatenate(xs, axis=0)
    cat1 = lambda xs: jnp.concatenate(xs, axis=1)
    sel = lambda a, hh: jnp.where(first_half, a, zero) if hh == 0 else jnp.where(first_half, zero, a)
    lo2 = lambda a: jnp.where(low2, a, zero2)
    hi2 = lambda a: jnp.where(low2, zero2, a)
    only = lambda a, i: jnp.where(blk == i, a, zero_h)
    block_diag4 = lambda a: cat0([only(a, i) for i in range(4)])

    prs = []
    for c in range(CHUNKS_PER_TILE):
        ts = slice(c * CHUNK, (c + 1) * CHUNK)
        for p in range(N_PAIR):
            ls = slice(p * PAIR, (p + 1) * PAIR)
            ld = lambda ref: ref[0, ts, ls]
            prs.append(dict(c=c, ts=ts, p=p, ls=ls, rt=ld(rt_ref), kt=ld(kt_ref), bt=ld(bt_ref),
                            at=ld(at_ref), v=v_ref[ts, ls], pend=pend_ref[0, c, :, ls]))
    for pr in prs:
        rhs = cat0([sel(pr["bt"], 0), sel(pr["kt"], 0), sel(pr["kt"], 1), sel(pr["bt"], 1)])
        aa = lax.dot_general(cat0([pr["at"], pr["rt"]]), rhs, _NT, preferred_element_type=F32)
        top = jnp.where(strict2, aa[0:CHUNK], 0.0)
        pr["top"] = top.astype(BF16)
        pr["bot"] = jnp.where(incl2, aa[CHUNK:], 0.0).astype(BF16)
        t0_, t1_ = top[:, :PAIR], top[:, PAIR:]
        pr["ppk"] = jnp.where(blk == 0, t0_[0:HALF], jnp.where(blk == 1, t0_[HALF:],
                              jnp.where(blk == 2, t1_[0:HALF], t1_[HALF:])))
        pr["ahl"] = jnp.where(blk == d, t0_[hi_rows], jnp.where(blk == 2 + d, t1_[hi_rows], 0.0)).astype(BF16)
        pr["spk"] = eye_pk
    yield
    for pr in prs:
        v, top = pr["v"], pr["top"]
        pr["akv"] = (_dot(top[:, :PAIR], cat0([zero, v])).astype(BF16),
                     _dot(top[:, PAIR:], cat0([v, zero])).astype(BF16))
    yield
    for _ in range(int(math.log2(HALF))):
        res = []
        for pr in prs:
            pb = pr["ppk"].astype(BF16)
            res.append(_dot(pb, cat1([block_diag4(pb), block_diag4(pr["spk"].astype(BF16))])))
        for pr, r in zip(prs, res):
            pr["ppk"] = r[:, :PAIR]
            pr["spk"] = pr["spk"] + r[:, PAIR:]
        yield
    res = []
    for pr in prs:
        sb = pr["spk"].astype(BF16)
        pr["sb"] = sb
        res.append(_dot(pr["ahl"], cat0([jnp.where(first_blk & (blk == i), sb, zero_h) for i in range(4)])))
    yield
    xs = [r.astype(BF16) for r in res]
    res = []
    for pr, xb in zip(prs, xs):
        rows = [zero_h] * 4
        rows[1 - d] = only(xb, d)
        rows[3 - d] = only(xb, 2 + d)
        res.append(_dot(jnp.where(first_blk, zero_h, pr["sb"]), cat0(rows)))
    yield
    for pr, thl in zip(prs, res):
        sb = pr["sb"]
        t_lo = jnp.where(first_blk, sb, zero_h)
        t_hi = jnp.where(first_blk, thl.astype(BF16), sb)
        tinv = cat0([t_lo, t_hi]) if d == 0 else cat0([t_hi, t_lo])
        w0 = cat1([sel(pr["at"], 0), sel(pr["akv"][0], 0)])
        w1 = cat1([sel(pr["at"], 1), sel(pr["akv"][1], 1)])
        pr["au"] = _dot(tinv, cat0([w0, w1])).astype(BF16)
        pr["zv"] = cat1([zero, pr["v"]])
    yield
    for pr in prs:
        au, zv = pr["au"], pr["zv"]
        pr["o"] = _dot(pr["bot"], cat0([lo2(au), lo2(zv), hi2(zv), hi2(au)]))
        bek = cat0([pr["bt"].astype(F32) * pr["pend"], pr["kt"].astype(F32) * pr["pend"]]).astype(BF16)
        pr["md"] = _dot(bek.T, cat0([au, zv]))
    yield
    for pr in prs:
        o, md, ts, ls = pr["o"], pr["md"], pr["ts"], pr["ls"]
        y0_scr[ts, ls] = o[:, PAIR:]
        mbd = jnp.where(eye_pair, pr["pend"], 0.0) + jnp.where(same_half, md[:, :PAIR], 0.0)
        qm_scr[pr["c"], pr["p"]] = cat0([(pr["rt"].astype(F32) + o[:, :PAIR]).astype(BF16), mbd.astype(BF16)])
        d_scr[pr["c"], pr["p"]] = jnp.where(same_half, md[:, PAIR:], 0.0)


def _scan_body(*refs):
    n_in = 6
    ins = (refs[0:n_in], refs[n_in:2 * n_in])
    ys = refs[2 * n_in:2 * n_in + 2]
    h_scr, y0_scr, qm_scr, d_scr = refs[2 * n_in + 2:]

    @pl.when(pl.program_id(1) == 0)
    def _():
        h_scr[...] = jnp.zeros_like(h_scr)

    stages = [_scan_factors(d, *ins[d], y0_scr.at[d], qm_scr.at[d], d_scr.at[d]) for d in range(N_DIR)]
    for _ in itertools.zip_longest(*stages):
        pass

    states = [h_scr[d, p] for d in range(N_DIR) for p in range(N_PAIR)]
    for ci in range(CHUNKS_PER_TILE):
        work = []
        for d in range(N_DIR):
            c = ci if d == 0 else CHUNKS_PER_TILE - 1 - ci
            for p in range(N_PAIR):
                work.append((d, c, slice(c * CHUNK, (c + 1) * CHUNK), p, slice(p * PAIR, (p + 1) * PAIR)))
        res = [_dot(qm_scr[d, c, p], h.astype(BF16)) for (d, c, ts, p, ls), h in zip(work, states)]
        states = [r[CHUNK:] + d_scr[d, c, p] for (d, c, ts, p, ls), r in zip(work, res)]
        for (d, c, ts, p, ls), r in zip(work, res):
            ys[d][ts, ls] = (y0_scr[d, ts, ls] + r[0:CHUNK]).astype(BF16)
    for i, h in enumerate(states):
        h_scr[i // N_PAIR, i % N_PAIR] = h


def _scan(rt, kt, bt, at, v, pend, b, seq):
    t = v.shape[0]
    tc = TOK_TILE
    nblk = seq // tc
    tok = (lambda bi, j: bi * nblk + j, lambda bi, j: bi * nblk + nblk - 1 - j)
    in_specs, args = [], []
    for d in range(N_DIR):
        dblk = pl.BlockSpec((1, tc, D_RWKV), lambda bi, j, d=d: (d, tok[d](bi, j), 0))
        in_specs += [dblk] * 4 + [
            pl.BlockSpec((tc, D_RWKV), lambda bi, j, d=d: (tok[d](bi, j), 0)),
            pl.BlockSpec((1, CHUNKS_PER_TILE, 1, D_RWKV), lambda bi, j, d=d: (d, tok[d](bi, j), 0, 0))]
        args += [rt, kt, bt, at, v, pend]
    return pl.pallas_call(
        _scan_body,
        grid=(b, nblk),
        in_specs=in_specs,
        out_specs=[pl.BlockSpec((tc, D_RWKV), lambda bi, j, d=d: (tok[d](bi, j), 0)) for d in range(N_DIR)],
        out_shape=[jax.ShapeDtypeStruct((t, D_RWKV), BF16)] * N_DIR,
        scratch_shapes=[pltpu.VMEM((N_DIR, N_PAIR, PAIR, PAIR), F32),
                        pltpu.VMEM((N_DIR, tc, D_RWKV), F32),
                        pltpu.VMEM((N_DIR, CHUNKS_PER_TILE, N_PAIR, CHUNK + PAIR, PAIR), BF16),
                        pltpu.VMEM((N_DIR, CHUNKS_PER_TILE, N_PAIR, PAIR, PAIR), F32)],
        compiler_params=_params(("parallel", "arbitrary")),
        name="rwkv_scan",
    )(*args)


def _merge_body(y0_ref, y1_ref, bonus_ref, g_ref, ona_ref, gt_ref, x_ref, seg_ref, lnw_ref, lnb_ref,
                wpa_ref, wpb_ref, wout_ref, npost_ref, o_ref):
    inv_n = 1.0 / HEAD_DIM
    y = y0_ref[...].astype(F32) + y1_ref[...]
    mean = _dot(y.astype(BF16), seg_ref[...]) * inv_n
    yc = y - mean
    var = _dot((yc * yc).astype(BF16), seg_ref[...]) * inv_n
    yn = yc * lax.rsqrt(var + GN_EPS) * lnw_ref[...] + lnb_ref[...]
    orw = ((yn + bonus_ref[...]) * g_ref[...]).astype(BF16)
    gates = gt_ref[...]
    merged = (gates[:, :D_MODEL] * _dot(ona_ref[...], wpa_ref[...])
              + gates[:, D_MODEL:] * _dot(orw, wpb_ref[...]))
    z = _dot(merged.astype(BF16), wout_ref[...])
    ms = jnp.mean(z * z, axis=-1, keepdims=True)
    o_ref[...] = x_ref[...] + z * lax.rsqrt(ms + RMS_EPS) * npost_ref[...]


def _merge(y0, y1, bonus, g, ona, gates, x, seg, lnw, lnb, wpa, wpb, wout, npost):
    t = x.shape[0]
    tm = TOK_TILE
    row = lambda w: pl.BlockSpec((tm, w), lambda i: (i, 0))
    consts = [seg, lnw, lnb, wpa, wpb, wout, npost]
    return pl.pallas_call(
        _merge_body,
        grid=(t // tm,),
        in_specs=[row(D_RWKV), row(D_RWKV), row(D_RWKV), row(D_RWKV), row(D_NA), row(2 * D_MODEL), row(D_MODEL)]
        + [_const_spec(c.shape) for c in consts],
        out_specs=row(D_MODEL),
        out_shape=jax.ShapeDtypeStruct((t, D_MODEL), F32),
        compiler_params=_params(("parallel",)),
        name="merge",
    )(y0, y1, bonus, g, ona, gates, x, *consts)


def _ffn_body(x_ref, npre_ref, wup_ref, wdn_ref, npost_ref, o_ref):
    x = x_ref[...]
    ms = jnp.mean(x * x, axis=-1, keepdims=True)
    h = (x * lax.rsqrt(ms + RMS_EPS) * npre_ref[...]).astype(BF16)
    u = jnp.maximum(_dot(h, wup_ref[...]), 0.0)
    f = _dot((u * u).astype(BF16), wdn_ref[...])
    ms = jnp.mean(f * f, axis=-1, keepdims=True)
    o_ref[...] = x + f * lax.rsqrt(ms + RMS_EPS) * npost_ref[...]


def _ffn(x, npre, wup, wdn, npost):
    t = x.shape[0]
    tm = TOK_TILE
    row = pl.BlockSpec((tm, D_MODEL), lambda i: (i, 0))
    consts = [npre, wup, wdn, npost]
    return pl.pallas_call(
        _ffn_body,
        grid=(t // tm,),
        in_specs=[row] + [_const_spec(c.shape) for c in consts],
        out_specs=row,
        out_shape=jax.ShapeDtypeStruct((t, D_MODEL), F32),
        compiler_params=_params(("parallel",)),
        name="ffn",
    )(x, *consts)


def _block_diag2(w):
    z = jnp.zeros_like(w[0])
    return jnp.concatenate([jnp.concatenate([w[0], z], axis=1),
                            jnp.concatenate([z, w[1]], axis=1)], axis=0)


def _layer_consts(w_in, b_gate, rpb, mu_prev, mu_next, w0, w2, a0, a2, g2, k_k, k_a, r_k,
                  ln_w, ln_b, w_proj_a, w_proj_b, w_out, w_up, w_down,
                  norm_pre_mix, norm_post_mix, norm_pre_ffn, norm_post_ffn):
    rowv = lambda a: a.reshape(1, -1).astype(F32)
    head = jnp.arange(D_RWKV) // HEAD_DIM
    idx = jnp.arange(CHUNK)
    tril = (idx[:, None] >= idx[None, :])
    return dict(
        wmain=jnp.concatenate([w_in[:, :3 * D_NA], w_in[:, 3 * D_NA + D_SHIFT:]], axis=1).astype(BF16),
        wsh=w_in[:, 3 * D_NA:3 * D_NA + D_SHIFT].astype(BF16),
        bg=rowv(b_gate), bias=_na_bias_table(rpb),
        mu0=rowv(1.0 - mu_prev - mu_next), mup=rowv(mu_prev), mun=rowv(mu_next),
        w0=rowv(w0), w2bd=_block_diag2(w2).astype(BF16),
        a0=rowv(a0), a2bd=_block_diag2(a2).astype(BF16),
        g2=g2.astype(BF16), kkw=rowv(k_k), ka=rowv(k_a), rk=rowv(r_k),
        seg=(head[:, None] == head[None, :]).astype(BF16),
        tri=jnp.stack([tril, tril.T]).astype(BF16),
        lnw=rowv(ln_w), lnb=rowv(ln_b),
        wpa=w_proj_a.astype(BF16), wpb=w_proj_b.astype(BF16), wout=w_out.astype(BF16),
        wup=w_up.astype(BF16), wdn=w_down.astype(BF16),
        npre_mix=rowv(norm_pre_mix), npost_mix=rowv(norm_post_mix),
        npre_ffn=rowv(norm_pre_ffn), npost_ffn=rowv(norm_post_ffn),
    )


def _encoder_layer(x, b, seq, c):
    q, k, v, sh, gates = _inproj(x, seq, c["npre_mix"], c["wmain"], c["wsh"], c["bg"],
                                 c["mu0"], c["mup"], c["mun"])
    ona = _natten(q, k, v, c["bias"], b, seq)
    rt, kt, bt, at, vr, pend, g, bonus = _prep(
        sh, c["w0"], c["w2bd"], c["a0"], c["a2bd"], c["g2"],
        c["kkw"], c["ka"], c["rk"], c["seg"], c["tri"])
    y0, y1 = _scan(rt, kt, bt, at, vr, pend, b, seq)
    x1 = _merge(y0, y1, bonus, g, ona, gates, x, c["seg"], c["lnw"], c["lnb"],
                c["wpa"], c["wpb"], c["wout"], c["npost_mix"])
    return _ffn(x1, c["npre_ffn"], c["wup"], c["wdn"], c["npost_ffn"])


def _trunk(x, layer_consts):
    b, seq, dm = x.shape
    assert dm == D_MODEL and seq % TOK_TILE == 0 and seq // GRID_W >= WIN_H
    h = x.reshape(b * seq, dm)
    for c in layer_consts:
        h = _encoder_layer(h, b, seq, c)
    return h.reshape(b, seq, dm)


def kernel(x_prompt, x_sample, w_in, b_gate, rpb, mu_prev, mu_next, w0, w2, a0, a2, g2, k_k, k_a, r_k, ln_w, ln_b, w_proj_a, w_proj_b, w_out, w_up, w_down, norm_pre_mix, norm_post_mix, norm_pre_ffn, norm_post_ffn):
    stacked = (w_in, b_gate, rpb, mu_prev, mu_next, w0, w2, a0, a2, g2, k_k, k_a, r_k,
               ln_w, ln_b, w_proj_a, w_proj_b, w_out, w_up, w_down,
               norm_pre_mix, norm_post_mix, norm_pre_ffn, norm_post_ffn)
    depth = w_in.shape[0]
    layer_consts = [_layer_consts(*[p[i] for p in stacked]) for i in range(depth)]
    return (_trunk(x_prompt, layer_consts), _trunk(x_sample, layer_consts))
```

```python
import functools
import itertools
import math

import jax
import jax.numpy as jnp
from jax import lax
from jax.experimental import pallas as pl
from jax.experimental.pallas import tpu as pltpu

F32 = jnp.float32
BF16 = jnp.bfloat16

D_MODEL = 1024
GRID_W = 64
WIN_H = 8
WIN_W = 16
NA_HEADS = 8
HEAD_DIM = 64
D_NA = NA_HEADS * HEAD_DIM
RWKV_HEADS = 8
D_RWKV = RWKV_HEADS * HEAD_DIM
DECAY_LORA = 64
ICLR_LORA = 64
GATE_LORA = 128
N_DIR = 2
D_FF = 4 * D_MODEL
D_SHIFT = 3 * D_RWKV + N_DIR * DECAY_LORA + N_DIR * ICLR_LORA + GATE_LORA
RMS_EPS = 1e-6
GN_EPS = 64e-5
KK_EPS = 1e-12
NEG_BIG = -1e30

CHUNK = 64
HALF = CHUNK // 2
assert CHUNK == HEAD_DIM
PAIR = 2 * HEAD_DIM
N_PAIR = D_RWKV // PAIR
TOK_TILE = 512
HALO = 8
NA_TILE = 1024
ROWS_PER_TILE = NA_TILE // GRID_W
CHUNKS_PER_TILE = TOK_TILE // CHUNK
NA_ROWS_PER_ITER = 1
V7X_VMEM_LIMIT = 56 * 1024 * 1024

_NT = (((1,), (1,)), ((), ()))


def _dot(a, b):
    return jnp.dot(a, b, preferred_element_type=F32)


def _sigmoid(x):
    return 0.5 * jnp.tanh(0.5 * x) + 0.5


def _params(sem):
    return pltpu.CompilerParams(dimension_semantics=sem, vmem_limit_bytes=V7X_VMEM_LIMIT)


def _const_spec(shape):
    nd = len(shape)
    return pl.BlockSpec(shape, lambda *_: (0,) * nd, pipeline_mode=pl.Buffered(1))


def _inproj_body(x_ref, xp_ref, xn_ref, g_ref, wmain_ref, wsh_ref, bg_ref,
                 mu0_ref, mup_ref, mun_ref, q_ref, k_ref, v_ref, sh_ref, gt_ref, *, tiles_per_seq):
    tm = TOK_TILE
    pos = lax.rem(pl.program_id(0), tiles_per_seq)
    xe = jnp.concatenate([xp_ref[...], x_ref[...], xn_ref[...]], axis=0)
    ms = jnp.mean(xe * xe, axis=-1, keepdims=True)
    he = xe * lax.rsqrt(ms + RMS_EPS) * g_ref[...]
    rowe = lax.broadcasted_iota(jnp.int32, (tm + 2 * HALO, 1), 0)
    inside = (((rowe >= HALO) | (pos != 0))
              & ((rowe < tm + HALO) | (pos != tiles_per_seq - 1)))
    he = jnp.where(inside, he, 0.0)
    main = slice(HALO, tm + HALO)
    h = he[main].astype(BF16)
    qkv_gt = _dot(h, wmain_ref[...])
    qkv = qkv_gt[:, 0:3 * D_NA]
    q_ref[...] = (qkv[:, 0:D_NA] * (HEAD_DIM ** -0.5)).astype(BF16)
    k_ref[...] = qkv[:, D_NA:2 * D_NA].astype(BF16)
    v_ref[...] = qkv[:, 2 * D_NA:3 * D_NA].astype(BF16)
    pe = _dot(he.astype(BF16), wsh_ref[...])
    sh_ref[...] = (mu0_ref[...] * pe[main] + mup_ref[...] * pe[HALO - 1:tm + HALO - 1]
                   + mun_ref[...] * pe[HALO + 1:tm + HALO + 1])
    gt_ref[...] = _sigmoid(qkv_gt[:, 3 * D_NA:] + bg_ref[...]).astype(BF16)


def _inproj(x, seq, g, wmain, wsh, bg, mu0, mup, mun):
    t = x.shape[0]
    tm = TOK_TILE
    nh = tm // HALO
    last = t // HALO - 1
    row = lambda w: pl.BlockSpec((tm, w), lambda i: (i, 0))
    consts = [g, wmain, wsh, bg, mu0, mup, mun]
    return pl.pallas_call(
        functools.partial(_inproj_body, tiles_per_seq=seq // tm),
        grid=(t // tm,),
        in_specs=[row(D_MODEL),
                  pl.BlockSpec((HALO, D_MODEL), lambda i: (jnp.maximum(i * nh - 1, 0), 0)),
                  pl.BlockSpec((HALO, D_MODEL), lambda i: (jnp.minimum((i + 1) * nh, last), 0))]
        + [_const_spec(c.shape) for c in consts],
        out_specs=[row(D_NA), row(D_NA), row(D_NA), row(D_SHIFT), row(2 * D_MODEL)],
        out_shape=[jax.ShapeDtypeStruct((t, D_NA), BF16)] * 3
        + [jax.ShapeDtypeStruct((t, D_SHIFT), F32), jax.ShapeDtypeStruct((t, 2 * D_MODEL), BF16)],
        compiler_params=_params(("parallel",)),
        name="inproj",
    )(x, x, x, *consts)


def _na_bias_table(rpb):
    c = jnp.arange(GRID_W)
    col_start = jnp.clip(c - WIN_W // 2, 0, GRID_W - WIN_W)
    col_ok = (c[None, :] >= col_start[:, None]) & (c[None, :] < col_start[:, None] + WIN_W)
    col_idx = jnp.clip(c[None, :] - c[:, None] + (WIN_W - 1), 0, 2 * WIN_W - 2)
    onehot = (col_idx[:, :, None] == jnp.arange(2 * WIN_W - 1)[None, None, :]).astype(F32)
    cols = jnp.einsum("hdc,qkc->hqdk", rpb.astype(F32), onehot,
                      precision=lax.Precision.HIGHEST)
    cols = jnp.where(col_ok[None, :, None, :], cols, NEG_BIG)
    tab = jnp.stack([cols[:, :, WIN_H - 1 - e:2 * WIN_H - 1 - e, :] for e in range(WIN_H)])
    return tab.reshape(WIN_H, NA_HEADS, GRID_W, WIN_H * GRID_W)


def _natten_body(q_ref, k_ref, v_ref, bias_ref, o_ref, *, rows):
    j = pl.program_id(1)
    first_half = lax.broadcasted_iota(jnp.int32, (GRID_W, PAIR), 1) < HEAD_DIM
    n_keys = WIN_H * GRID_W
    zero_q = jnp.zeros((GRID_W, PAIR), BF16)

    def problems(row_ids):
        work = []
        for jr in row_ids:
            r = j * ROWS_PER_TILE + jr
            rs = jnp.clip(r - WIN_H // 2, 0, rows - WIN_H)
            off = pl.multiple_of(rs * GRID_W, GRID_W)
            for p in range(N_PAIR):
                work.append((r - rs, off, jr * GRID_W, p, slice(p * PAIR, (p + 1) * PAIR)))
        return work

    def qk(work):
        scores = []
        for e, off, q0, p, ls in work:
            qp = q_ref[q0:q0 + GRID_W, ls]
            q2 = jnp.concatenate([jnp.where(first_half, qp, zero_q),
                                  jnp.where(first_half, zero_q, qp)], axis=0)
            scores.append(lax.dot_general(q2, k_ref[pl.ds(off, n_keys), ls], _NT,
                                          preferred_element_type=F32))
        return scores

    def softmax(work, scores):
        probs, sums = [], []
        for (e, off, q0, p, ls), s2 in zip(work, scores):
            pr2 = []
            for hh in range(2):
                s = s2[hh * GRID_W:(hh + 1) * GRID_W] + bias_ref[e, 2 * p + hh]
                pr = jnp.exp(s - jnp.max(s, axis=-1, keepdims=True))
                sums.append(jnp.sum(pr, axis=-1, keepdims=True))
                pr2.append(pr.astype(BF16))
            probs.append(jnp.concatenate(pr2, axis=0))
        return probs, sums

    def pv(work, probs, sums):
        outs = [_dot(pr2, v_ref[pl.ds(off, n_keys), ls]) for (e, off, q0, p, ls), pr2 in zip(work, probs)]
        for i, ((e, off, q0, p, ls), o2) in enumerate(zip(work, outs)):
            o0 = o2[0:GRID_W] / sums[2 * i]
            o1 = o2[GRID_W:] / sums[2 * i + 1]
            o_ref[q0:q0 + GRID_W, ls] = jnp.where(first_half, o0, o1).astype(BF16)

    groups = [problems(range(g, g + NA_ROWS_PER_ITER)) for g in range(0, ROWS_PER_TILE, NA_ROWS_PER_ITER)]
    scores = qk(groups[0])
    for gi, work in enumerate(groups):
        nxt_scores = qk(groups[gi + 1]) if gi + 1 < len(groups) else None
        probs, sums = softmax(work, scores)
        pv(work, probs, sums)
        scores = nxt_scores


def _natten(q, k, v, bias, b, seq):
    t = q.shape[0]
    rows = seq // GRID_W
    nblk = seq // NA_TILE
    blk = pl.BlockSpec((NA_TILE, D_NA), lambda bi, j: (bi * nblk + j, 0))
    whole = pl.BlockSpec((seq, D_NA), lambda bi, j: (bi, 0))
    return pl.pallas_call(
        functools.partial(_natten_body, rows=rows),
        grid=(b, nblk),
        in_specs=[blk, whole, whole, _const_spec(bias.shape)],
        out_specs=blk,
        out_shape=jax.ShapeDtypeStruct((t, D_NA), BF16),
        compiler_params=_params(("parallel", "arbitrary")),
        name="natten",
    )(q, k, v, bias)


def _split2_dot(tri, x):
    x0 = x.astype(BF16)
    x1 = (x - x0.astype(F32)).astype(BF16)
    return _dot(tri, x0) + _dot(tri, x1)


def _prep_body(sh_ref, w0_ref, w2_ref, a0_ref, a2_ref, g2_ref,
               kkw_ref, ka_ref, ka1_ref, rk_ref, seg_ref, tri_ref,
               rt_ref, kt_ref, bt_ref, at_ref, v_ref, pend_ref, g_ref, bonus_ref):
    sh = sh_ref[...]
    r = sh[:, 0:D_RWKV]
    k = sh[:, D_RWKV:2 * D_RWKV]
    v = sh[:, 2 * D_RWKV:3 * D_RWKV]
    off = 3 * D_RWKV
    wl = sh[:, off:off + 2 * DECAY_LORA]
    al = sh[:, off + 2 * DECAY_LORA:off + 2 * DECAY_LORA + 2 * ICLR_LORA]
    gl = sh[:, off + 2 * DECAY_LORA + 2 * ICLR_LORA:]

    zw = w0_ref[...] + _dot(jnp.tanh(wl).astype(BF16), w2_ref[...])
    za = a0_ref[...] + _dot(al.astype(BF16), a2_ref[...])
    g_ref[...] = _dot(_sigmoid(gl).astype(BF16), g2_ref[...]).astype(BF16)
    kkv = k * kkw_ref[...]
    kkn = kkv * lax.rsqrt(_dot((kkv * kkv).astype(BF16), seg_ref[...]) + KK_EPS)
    v_ref[...] = v.astype(BF16)
    r_rk = r * rk_ref[...]
    neg_kkn = -kkn
    half_c = -0.5 * math.exp(-0.5) * math.log2(math.e)

    bonus_parts = []
    for c in range(CHUNKS_PER_TILE):
        cs = slice(c * CHUNK, (c + 1) * CHUNK)
        bonus_c = jnp.zeros((CHUNK, D_RWKV), F32)
        for d in range(N_DIR):
            ds_ = slice(d * D_RWKV, (d + 1) * D_RWKV)
            icl = _sigmoid(za[cs, ds_])
            kd = k[cs] * (ka1_ref[...] + icl * ka_ref[...])
            bd = kkn[cs] * icl
            bonus_c = bonus_c + r_rk[cs] * kd
            x = half_c * jnp.tanh(0.5 * zw[cs, ds_]) + half_c
            lp = _split2_dot(tri_ref[d], x)
            lend = lp[CHUNK - 1:CHUNK] if d == 0 else lp[0:1]
            pw = jnp.exp2(lp)
            pinv = 1.0 / pw
            pex = jnp.exp2(lp - x)
            rt_ref[d, cs, :] = (r[cs] * pw).astype(BF16)
            kt_ref[d, cs, :] = (kd * pinv).astype(BF16)
            bt_ref[d, cs, :] = (bd * pinv).astype(BF16)
            at_ref[d, cs, :] = (neg_kkn[cs] * pex).astype(BF16)
            pend_ref[d, c] = jnp.exp2(lend)
        bonus_parts.append(bonus_c)
    bonus_acc = jnp.concatenate(bonus_parts, axis=0)
    bonus_ref[...] = (_dot(bonus_acc.astype(BF16), seg_ref[...]) * v).astype(BF16)


def _prep(sh, w0, w2bd, a0, a2bd, g2, kkw, ka, ka1, rk, seg, tri):
    t = sh.shape[0]
    tb = TOK_TILE
    row = lambda w: pl.BlockSpec((tb, w), lambda i: (i, 0))
    drow = pl.BlockSpec((N_DIR, tb, D_RWKV), lambda i: (0, i, 0))
    consts = [w0, w2bd, a0, a2bd, g2, kkw, ka, ka1, rk, seg, tri]
    bf = jax.ShapeDtypeStruct((N_DIR, t, D_RWKV), BF16)
    return pl.pallas_call(
        _prep_body,
        grid=(t // tb,),
        in_specs=[row(D_SHIFT)] + [_const_spec(c.shape) for c in consts],
        out_specs=[drow] * 4 + [row(D_RWKV),
                                pl.BlockSpec((N_DIR, CHUNKS_PER_TILE, 1, D_RWKV), lambda i: (0, i, 0, 0)),
                                row(D_RWKV), row(D_RWKV)],
        out_shape=[bf] * 4 + [jax.ShapeDtypeStruct((t, D_RWKV), BF16),
                              jax.ShapeDtypeStruct((N_DIR, t // CHUNK, 1, D_RWKV), F32),
                              jax.ShapeDtypeStruct((t, D_RWKV), BF16),
                              jax.ShapeDtypeStruct((t, D_RWKV), BF16)],
        compiler_params=_params(("parallel",)),
        name="rwkv_prep",
    )(sh, *consts)


def _scan_factors(d, rt_ref, kt_ref, bt_ref, at_ref, v_ref, pend_ref, y0_scr, qm_scr, d_scr):
    sign = 1 - 2 * d
    rowi2 = lax.broadcasted_iota(jnp.int32, (CHUNK, 2 * PAIR), 0)
    lane2 = lax.broadcasted_iota(jnp.int32, (CHUNK, 2 * PAIR), 1)
    diff = (rowi2 - (lane2 & (HEAD_DIM - 1))) * sign
    strict2 = diff > 0
    incl2 = diff >= 0
    low2 = (lane2 & HEAD_DIM) == 0
    first_half = lax.broadcasted_iota(jnp.int32, (CHUNK, PAIR), 1) < HEAD_DIM
    rowp = lax.broadcasted_iota(jnp.int32, (PAIR, PAIR), 0)
    lanep = lax.broadcasted_iota(jnp.int32, (PAIR, PAIR), 1)
    same_half = (rowp < HEAD_DIM) == (lanep < HEAD_DIM)
    eye_pair = rowp == lanep
    rowh = lax.broadcasted_iota(jnp.int32, (HALF, PAIR), 0)
    blk = lax.broadcasted_iota(jnp.int32, (HALF, PAIR), 1) // HALF
    first_blk = (blk & 1) == d
    eye_pk = ((lax.broadcasted_iota(jnp.int32, (HALF, PAIR), 1) & (HALF - 1)) == rowh).astype(F32)
    hi_rows = slice((1 - d) * HALF, (2 - d) * HALF)
    zero = jnp.zeros((CHUNK, PAIR), BF16)
    zero2 = jnp.zeros((CHUNK, 2 * PAIR), BF16)
    zero_h = jnp.zeros((HALF, PAIR), BF16)
    cat0 = lambda xs: jnp.concatenate(xs, axis=0)
    cat1 = lambda xs: jnp.concatenate(xs, axis=1)
    sel = lambda a, hh: jnp.where(first_half, a, zero) if hh == 0 else jnp.where(first_half, zero, a)
    lo2 = lambda a: jnp.where(low2, a, zero2)
    hi2 = lambda a: jnp.where(low2, zero2, a)
    only = lambda a, i: jnp.where(blk == i, a, zero_h)
    block_diag4 = lambda a: cat0([only(a, i) for i in range(4)])

    prs = []
    for c in range(CHUNKS_PER_TILE):
        ts = slice(c * CHUNK, (c + 1) * CHUNK)
        for p in range(N_PAIR):
            ls = slice(p * PAIR, (p + 1) * PAIR)
            ld = lambda ref: ref[0, ts, ls]
            prs.append(dict(c=c, ts=ts, p=p, ls=ls, rt=ld(rt_ref), kt=ld(kt_ref), bt=ld(bt_ref),
                            at=ld(at_ref), v=v_ref[ts, ls], pend=pend_ref[0, c, :, ls]))
    for pr in prs:
        rhs = cat0([sel(pr["bt"], 0), sel(pr["kt"], 0), sel(pr["kt"], 1), sel(pr["bt"], 1)])
        aa = lax.dot_general(cat0([pr["at"], pr["rt"]]), rhs, _NT, preferred_element_type=F32)
        top = jnp.where(strict2, aa[0:CHUNK], 0.0)
        pr["top"] = top.astype(BF16)
        pr["bot"] = jnp.where(incl2, aa[CHUNK:], 0.0).astype(BF16)
        t0_, t1_ = top[:, :PAIR], top[:, PAIR:]
        pr["ppk"] = jnp.where(blk == 0, t0_[0:HALF], jnp.where(blk == 1, t0_[HALF:],
                              jnp.where(blk == 2, t1_[0:HALF], t1_[HALF:])))
        pr["ahl"] = jnp.where(blk == d, t0_[hi_rows], jnp.where(blk == 2 + d, t1_[hi_rows], 0.0)).astype(BF16)
        pr["spk"] = eye_pk
    yield
    for pr in prs:
        v, top = pr["v"], pr["top"]
        pr["akv"] = (_dot(top[:, :PAIR], cat0([zero, v])).astype(BF16),
                     _dot(top[:, PAIR:], cat0([v, zero])).astype(BF16))
    yield
    for _ in range(int(math.log2(HALF))):
        res = []
        for pr in prs:
            pb = pr["ppk"].astype(BF16)
            res.append(_dot(pb, cat1([block_diag4(pb), block_diag4(pr["spk"].astype(BF16))])))
        for pr, r in zip(prs, res):
            pr["ppk"] = r[:, :PAIR]
            pr["spk"] = pr["spk"] + r[:, PAIR:]
        yield
    res = []
    for pr in prs:
        sb = pr["spk"].astype(BF16)
        pr["sb"] = sb
        res.append(_dot(pr["ahl"], cat0([jnp.where(first_blk & (blk == i), sb, zero_h) for i in range(4)])))
    yield
    xs = [r.astype(BF16) for r in res]
    res = []
    for pr, xb in zip(prs, xs):
        rows = [zero_h] * 4
        rows[1 - d] = only(xb, d)
        rows[3 - d] = only(xb, 2 + d)
        res.append(_dot(jnp.where(first_blk, zero_h, pr["sb"]), cat0(rows)))
    yield
    for pr, thl in zip(prs, res):
        sb = pr["sb"]
        t_lo = jnp.where(first_blk, sb, zero_h)
        t_hi = jnp.where(first_blk, thl.astype(BF16), sb)
        tinv = cat0([t_lo, t_hi]) if d == 0 else cat0([t_hi, t_lo])
        w0 = cat1([sel(pr["at"], 0), sel(pr["akv"][0], 0)])
        w1 = cat1([sel(pr["at"], 1), sel(pr["akv"][1], 1)])
        pr["au"] = _dot(tinv, cat0([w0, w1])).astype(BF16)
        pr["zv"] = cat1([zero, pr["v"]])
    yield
    for pr in prs:
        au, zv = pr["au"], pr["zv"]
        pr["o"] = _dot(pr["bot"], cat0([lo2(au), lo2(zv), hi2(zv), hi2(au)]))
        bek = cat0([pr["bt"].astype(F32) * pr["pend"], pr["kt"].astype(F32) * pr["pend"]]).astype(BF16)
        pr["md"] = _dot(bek.T, cat0([au, zv]))
    yield
    for pr in prs:
        o, md, ts, ls = pr["o"], pr["md"], pr["ts"], pr["ls"]
        y0_scr[ts, ls] = o[:, PAIR:]
        mbd = jnp.where(eye_pair, pr["pend"], 0.0) + jnp.where(same_half, md[:, :PAIR], 0.0)
        qm_scr[pr["c"], pr["p"]] = cat0([(pr["rt"].astype(F32) + o[:, :PAIR]).astype(BF16), mbd.astype(BF16)])
        d_scr[pr["c"], pr["p"]] = jnp.where(same_half, md[:, PAIR:], 0.0)


def _scan_body(*refs):
    n_in = 6
    ins = (refs[0:n_in], refs[n_in:2 * n_in])
    ys = refs[2 * n_in:2 * n_in + 2]
    h_scr, y0_scr, qm_scr, d_scr = refs[2 * n_in + 2:]

    @pl.when(pl.program_id(1) == 0)
    def _():
        h_scr[...] = jnp.zeros_like(h_scr)

    stages = [_scan_factors(d, *ins[d], y0_scr.at[d], qm_scr.at[d], d_scr.at[d]) for d in range(N_DIR)]
    for _ in itertools.zip_longest(*stages):
        pass

    states = [h_scr[d, p] for d in range(N_DIR) for p in range(N_PAIR)]
    for ci in range(CHUNKS_PER_TILE):
        work = []
        for d in range(N_DIR):
            c = ci if d == 0 else CHUNKS_PER_TILE - 1 - ci
            for p in range(N_PAIR):
                work.append((d, c, slice(c * CHUNK, (c + 1) * CHUNK), p, slice(p * PAIR, (p + 1) * PAIR)))
        res = [_dot(qm_scr[d, c, p], h.astype(BF16)) for (d, c, ts, p, ls), h in zip(work, states)]
        states = [r[CHUNK:] + d_scr[d, c, p] for (d, c, ts, p, ls), r in zip(work, res)]
        for (d, c, ts, p, ls), r in zip(work, res):
            ys[d][ts, ls] = (y0_scr[d, ts, ls] + r[0:CHUNK]).astype(BF16)
    for i, h in enumerate(states):
        h_scr[i // N_PAIR, i % N_PAIR] = h


def _scan(rt, kt, bt, at, v, pend, b, seq):
    t = v.shape[0]
    tc = TOK_TILE
    nblk = seq // tc
    tok = (lambda bi, j: bi * nblk + j, lambda bi, j: bi * nblk + nblk - 1 - j)
    in_specs, args = [], []
    for d in range(N_DIR):
        dblk = pl.BlockSpec((1, tc, D_RWKV), lambda bi, j, d=d: (d, tok[d](bi, j), 0))
        in_specs += [dblk] * 4 + [
            pl.BlockSpec((tc, D_RWKV), lambda bi, j, d=d: (tok[d](bi, j), 0)),
            pl.BlockSpec((1, CHUNKS_PER_TILE, 1, D_RWKV), lambda bi, j, d=d: (d, tok[d](bi, j), 0, 0))]
        args += [rt, kt, bt, at, v, pend]
    return pl.pallas_call(
        _scan_body,
        grid=(b, nblk),
        in_specs=in_specs,
        out_specs=[pl.BlockSpec((tc, D_RWKV), lambda bi, j, d=d: (tok[d](bi, j), 0)) for d in range(N_DIR)],
        out_shape=[jax.ShapeDtypeStruct((t, D_RWKV), BF16)] * N_DIR,
        scratch_shapes=[pltpu.VMEM((N_DIR, N_PAIR, PAIR, PAIR), F32),
                        pltpu.VMEM((N_DIR, tc, D_RWKV), F32),
                        pltpu.VMEM((N_DIR, CHUNKS_PER_TILE, N_PAIR, CHUNK + PAIR, PAIR), BF16),
                        pltpu.VMEM((N_DIR, CHUNKS_PER_TILE, N_PAIR, PAIR, PAIR), F32)],
        compiler_params=_params(("parallel", "arbitrary")),
        name="rwkv_scan",
    )(*args)


def _merge_body(y0_ref, y1_ref, bonus_ref, g_ref, ona_ref, gt_ref, x_ref, seg_ref, lnw_ref, lnb_ref,
                wpa_ref, wpb_ref, wout_ref, npost_ref, o_ref):
    inv_n = 1.0 / HEAD_DIM
    y = y0_ref[...].astype(F32) + y1_ref[...]
    mean = _dot(y.astype(BF16), seg_ref[...]) * inv_n
    yc = y - mean
    var = _dot((yc * yc).astype(BF16), seg_ref[...]) * inv_n
    yn = yc * lax.rsqrt(var + GN_EPS) * lnw_ref[...] + lnb_ref[...]
    orw = ((yn + bonus_ref[...]) * g_ref[...]).astype(BF16)
    gates = gt_ref[...]
    merged = (gates[:, :D_MODEL] * _dot(ona_ref[...], wpa_ref[...])
              + gates[:, D_MODEL:] * _dot(orw, wpb_ref[...]))
    z = _dot(merged.astype(BF16), wout_ref[...])
    ms = jnp.mean(z * z, axis=-1, keepdims=True)
    o_ref[...] = x_ref[...] + z * lax.rsqrt(ms + RMS_EPS) * npost_ref[...]


def _merge(y0, y1, bonus, g, ona, gates, x, seg, lnw, lnb, wpa, wpb, wout, npost):
    t = x.shape[0]
    tm = TOK_TILE
    row = lambda w: pl.BlockSpec((tm, w), lambda i: (i, 0))
    consts = [seg, lnw, lnb, wpa, wpb, wout, npost]
    return pl.pallas_call(
        _merge_body,
        grid=(t // tm,),
        in_specs=[row(D_RWKV), row(D_RWKV), row(D_RWKV), row(D_RWKV), row(D_NA), row(2 * D_MODEL), row(D_MODEL)]
        + [_const_spec(c.shape) for c in consts],
        out_specs=row(D_MODEL),
        out_shape=jax.ShapeDtypeStruct((t, D_MODEL), F32),
        compiler_params=_params(("parallel",)),
        name="merge",
    )(y0, y1, bonus, g, ona, gates, x, *consts)


def _ffn_body(x_ref, npre_ref, wup_ref, wdn_ref, npost_ref, o_ref):
    x = x_ref[...]
    ms = jnp.mean(x * x, axis=-1, keepdims=True)
    h = (x * lax.rsqrt(ms + RMS_EPS) * npre_ref[...]).astype(BF16)
    u = jnp.maximum(_dot(h, wup_ref[...]), 0.0)
    f = _dot((u * u).astype(BF16), wdn_ref[...])
    ms = jnp.mean(f * f, axis=-1, keepdims=True)
    o_ref[...] = x + f * lax.rsqrt(ms + RMS_EPS) * npost_ref[...]


def _ffn(x, npre, wup, wdn, npost):
    t = x.shape[0]
    tm = TOK_TILE
    row = pl.BlockSpec((tm, D_MODEL), lambda i: (i, 0))
    consts = [npre, wup, wdn, npost]
    return pl.pallas_call(
        _ffn_body,
        grid=(t // tm,),
        in_specs=[row] + [_const_spec(c.shape) for c in consts],
        out_specs=row,
        out_shape=jax.ShapeDtypeStruct((t, D_MODEL), F32),
        compiler_params=_params(("parallel",)),
        name="ffn",
    )(x, *consts)


def _block_diag2(w):
    z = jnp.zeros_like(w[0])
    return jnp.concatenate([jnp.concatenate([w[0], z], axis=1),
                            jnp.concatenate([z, w[1]], axis=1)], axis=0)


def _layer_consts(w_in, b_gate, rpb, mu_prev, mu_next, w0, w2, a0, a2, g2, k_k, k_a, r_k,
                  ln_w, ln_b, w_proj_a, w_proj_b, w_out, w_up, w_down,
                  norm_pre_mix, norm_post_mix, norm_pre_ffn, norm_post_ffn):
    rowv = lambda a: a.reshape(1, -1).astype(F32)
    head = jnp.arange(D_RWKV) // HEAD_DIM
    idx = jnp.arange(CHUNK)
    tril = (idx[:, None] >= idx[None, :])
    return dict(
        wmain=jnp.concatenate([w_in[:, :3 * D_NA], w_in[:, 3 * D_NA + D_SHIFT:]], axis=1).astype(BF16),
        wsh=w_in[:, 3 * D_NA:3 * D_NA + D_SHIFT].astype(BF16),
        bg=rowv(b_gate), bias=_na_bias_table(rpb),
        mu0=rowv(1.0 - mu_prev - mu_next), mup=rowv(mu_prev), mun=rowv(mu_next),
        w0=rowv(w0), w2bd=_block_diag2(w2).astype(BF16),
        a0=rowv(a0), a2bd=_block_diag2(a2).astype(BF16),
        g2=g2.astype(BF16), kkw=rowv(k_k), ka=rowv(k_a), ka1=rowv(1.0 - k_a), rk=rowv(r_k),
        seg=(head[:, None] == head[None, :]).astype(BF16),
        tri=jnp.stack([tril, tril.T]).astype(BF16),
        lnw=rowv(ln_w), lnb=rowv(ln_b),
        wpa=w_proj_a.astype(BF16), wpb=w_proj_b.astype(BF16), wout=w_out.astype(BF16),
        wup=w_up.astype(BF16), wdn=w_down.astype(BF16),
        npre_mix=rowv(norm_pre_mix), npost_mix=rowv(norm_post_mix),
        npre_ffn=rowv(norm_pre_ffn), npost_ffn=rowv(norm_post_ffn),
    )


def _encoder_layer(x, b, seq, c):
    q, k, v, sh, gates = _inproj(x, seq, c["npre_mix"], c["wmain"], c["wsh"], c["bg"],
                                 c["mu0"], c["mup"], c["mun"])
    ona = _natten(q, k, v, c["bias"], b, seq)
    rt, kt, bt, at, vr, pend, g, bonus = _prep(
        sh, c["w0"], c["w2bd"], c["a0"], c["a2bd"], c["g2"],
        c["kkw"], c["ka"], c["ka1"], c["rk"], c["seg"], c["tri"])
    y0, y1 = _scan(rt, kt, bt, at, vr, pend, b, seq)
    x1 = _merge(y0, y1, bonus, g, ona, gates, x, c["seg"], c["lnw"], c["lnb"],
                c["wpa"], c["wpb"], c["wout"], c["npost_mix"])
    return _ffn(x1, c["npre_ffn"], c["wup"], c["wdn"], c["npost_ffn"])


def _trunk(x, layer_consts):
    b, seq, dm = x.shape
    assert dm == D_MODEL and seq % TOK_TILE == 0 and seq % NA_TILE == 0 and seq // GRID_W >= WIN_H
    h = x.reshape(b * seq, dm)
    for c in layer_consts:
        h = _encoder_layer(h, b, seq, c)
    return h.reshape(b, seq, dm)


def kernel(x_prompt, x_sample, w_in, b_gate, rpb, mu_prev, mu_next, w0, w2, a0, a2, g2, k_k, k_a, r_k, ln_w, ln_b, w_proj_a, w_proj_b, w_out, w_up, w_down, norm_pre_mix, norm_post_mix, norm_pre_ffn, norm_post_ffn):
    stacked = (w_in, b_gate, rpb, mu_prev, mu_next, w0, w2, a0, a2, g2, k_k, k_a, r_k,
               ln_w, ln_b, w_proj_a, w_proj_b, w_out, w_up, w_down,
               norm_pre_mix, norm_post_mix, norm_pre_ffn, norm_post_ffn)
    depth = w_in.shape[0]
    layer_consts = [_layer_consts(*[p[i] for p in stacked]) for i in range(depth)]
    return (_trunk(x_prompt, layer_consts), _trunk(x_sample, layer_consts))
```

```python
import functools
import itertools
import math

import jax
import jax.numpy as jnp
from jax import lax
from jax.experimental import pallas as pl
from jax.experimental.pallas import tpu as pltpu

F32 = jnp.float32
BF16 = jnp.bfloat16

D_MODEL = 1024
GRID_W = 64
WIN_H = 8
WIN_W = 16
NA_HEADS = 8
HEAD_DIM = 64
D_NA = NA_HEADS * HEAD_DIM
RWKV_HEADS = 8
D_RWKV = RWKV_HEADS * HEAD_DIM
DECAY_LORA = 64
ICLR_LORA = 64
GATE_LORA = 128
N_DIR = 2
D_FF = 4 * D_MODEL
D_SHIFT = 3 * D_RWKV + N_DIR * DECAY_LORA + N_DIR * ICLR_LORA + GATE_LORA
RMS_EPS = 1e-6
GN_EPS = 64e-5
KK_EPS = 1e-12
NEG_BIG = -1e30

CHUNK = 64
HALF = CHUNK // 2
assert CHUNK == HEAD_DIM
PAIR = 2 * HEAD_DIM
N_PAIR = D_RWKV // PAIR
TOK_TILE = 512
HALO = 8
NA_TILE = 1024
FFN_TILE = 1024
FFN_SPLIT = 4
ROWS_PER_TILE = NA_TILE // GRID_W
CHUNKS_PER_TILE = TOK_TILE // CHUNK
NA_ROWS_PER_ITER = 1
V7X_VMEM_LIMIT = 56 * 1024 * 1024

_NT = (((1,), (1,)), ((), ()))


def _dot(a, b):
    return jnp.dot(a, b, preferred_element_type=F32)


def _sigmoid(x):
    return 0.5 * jnp.tanh(0.5 * x) + 0.5


def _params(sem):
    return pltpu.CompilerParams(dimension_semantics=sem, vmem_limit_bytes=V7X_VMEM_LIMIT)


def _const_spec(shape):
    nd = len(shape)
    return pl.BlockSpec(shape, lambda *_: (0,) * nd, pipeline_mode=pl.Buffered(1))


def _inproj_body(x_ref, xp_ref, xn_ref, g_ref, wmain_ref, wsh_ref, bg_ref,
                 mu0_ref, mup_ref, mun_ref, q_ref, k_ref, v_ref, sh_ref, gt_ref, *, tiles_per_seq):
    tm = TOK_TILE
    pos = lax.rem(pl.program_id(0), tiles_per_seq)
    xe = jnp.concatenate([xp_ref[...], x_ref[...], xn_ref[...]], axis=0)
    ms = jnp.mean(xe * xe, axis=-1, keepdims=True)
    he = xe * lax.rsqrt(ms + RMS_EPS) * g_ref[...]
    rowe = lax.broadcasted_iota(jnp.int32, (tm + 2 * HALO, 1), 0)
    inside = (((rowe >= HALO) | (pos != 0))
              & ((rowe < tm + HALO) | (pos != tiles_per_seq - 1)))
    he = jnp.where(inside, he, 0.0)
    main = slice(HALO, tm + HALO)
    h = he[main].astype(BF16)
    qkv_gt = _dot(h, wmain_ref[...])
    qkv = qkv_gt[:, 0:3 * D_NA]
    q_ref[...] = (qkv[:, 0:D_NA] * (HEAD_DIM ** -0.5)).astype(BF16)
    k_ref[...] = qkv[:, D_NA:2 * D_NA].astype(BF16)
    v_ref[...] = qkv[:, 2 * D_NA:3 * D_NA].astype(BF16)
    pe = _dot(he.astype(BF16), wsh_ref[...])
    sh_ref[...] = (mu0_ref[...] * pe[main] + mup_ref[...] * pe[HALO - 1:tm + HALO - 1]
                   + mun_ref[...] * pe[HALO + 1:tm + HALO + 1])
    gt_ref[...] = _sigmoid(qkv_gt[:, 3 * D_NA:] + bg_ref[...]).astype(BF16)


def _inproj(x, seq, g, wmain, wsh, bg, mu0, mup, mun):
    t = x.shape[0]
    tm = TOK_TILE
    nh = tm // HALO
    last = t // HALO - 1
    row = lambda w: pl.BlockSpec((tm, w), lambda i: (i, 0))
    consts = [g, wmain, wsh, bg, mu0, mup, mun]
    return pl.pallas_call(
        functools.partial(_inproj_body, tiles_per_seq=seq // tm),
        grid=(t // tm,),
        in_specs=[row(D_MODEL),
                  pl.BlockSpec((HALO, D_MODEL), lambda i: (jnp.maximum(i * nh - 1, 0), 0)),
                  pl.BlockSpec((HALO, D_MODEL), lambda i: (jnp.minimum((i + 1) * nh, last), 0))]
        + [_const_spec(c.shape) for c in consts],
        out_specs=[row(D_NA), row(D_NA), row(D_NA), row(D_SHIFT), row(2 * D_MODEL)],
        out_shape=[jax.ShapeDtypeStruct((t, D_NA), BF16)] * 3
        + [jax.ShapeDtypeStruct((t, D_SHIFT), F32), jax.ShapeDtypeStruct((t, 2 * D_MODEL), BF16)],
        compiler_params=_params(("parallel",)),
        name="inproj",
    )(x, x, x, *consts)


def _na_bias_table(rpb):
    c = jnp.arange(GRID_W)
    col_start = jnp.clip(c - WIN_W // 2, 0, GRID_W - WIN_W)
    col_ok = (c[None, :] >= col_start[:, None]) & (c[None, :] < col_start[:, None] + WIN_W)
    col_idx = jnp.clip(c[None, :] - c[:, None] + (WIN_W - 1), 0, 2 * WIN_W - 2)
    onehot = (col_idx[:, :, None] == jnp.arange(2 * WIN_W - 1)[None, None, :]).astype(F32)
    cols = jnp.einsum("hdc,qkc->hqdk", rpb.astype(F32), onehot,
                      precision=lax.Precision.HIGHEST)
    cols = jnp.where(col_ok[None, :, None, :], cols, NEG_BIG)
    tab = jnp.stack([cols[:, :, WIN_H - 1 - e:2 * WIN_H - 1 - e, :] for e in range(WIN_H)])
    return tab.reshape(WIN_H, NA_HEADS, GRID_W, WIN_H * GRID_W)


def _natten_body(q_ref, k_ref, v_ref, bias_ref, o_ref, *, rows):
    j = pl.program_id(1)
    first_half = lax.broadcasted_iota(jnp.int32, (GRID_W, PAIR), 1) < HEAD_DIM
    n_keys = WIN_H * GRID_W
    zero_q = jnp.zeros((GRID_W, PAIR), BF16)

    def problems(row_ids):
        work = []
        for jr in row_ids:
            r = j * ROWS_PER_TILE + jr
            rs = jnp.clip(r - WIN_H // 2, 0, rows - WIN_H)
            off = pl.multiple_of(rs * GRID_W, GRID_W)
            for p in range(N_PAIR):
                work.append((r - rs, off, jr * GRID_W, p, slice(p * PAIR, (p + 1) * PAIR)))
        return work

    def qk(work):
        scores = []
        for e, off, q0, p, ls in work:
            qp = q_ref[q0:q0 + GRID_W, ls]
            q2 = jnp.concatenate([jnp.where(first_half, qp, zero_q),
                                  jnp.where(first_half, zero_q, qp)], axis=0)
            scores.append(lax.dot_general(q2, k_ref[pl.ds(off, n_keys), ls], _NT,
                                          preferred_element_type=F32))
        return scores

    def softmax(work, scores):
        probs, sums = [], []
        for (e, off, q0, p, ls), s2 in zip(work, scores):
            pr2 = []
            for hh in range(2):
                s = s2[hh * GRID_W:(hh + 1) * GRID_W] + bias_ref[e, 2 * p + hh]
                pr = jnp.exp(s - jnp.max(s, axis=-1, keepdims=True))
                sums.append(jnp.sum(pr, axis=-1, keepdims=True))
                pr2.append(pr.astype(BF16))
            probs.append(jnp.concatenate(pr2, axis=0))
        return probs, sums

    def pv(work, probs, sums):
        outs = [_dot(pr2, v_ref[pl.ds(off, n_keys), ls]) for (e, off, q0, p, ls), pr2 in zip(work, probs)]
        for i, ((e, off, q0, p, ls), o2) in enumerate(zip(work, outs)):
            o0 = o2[0:GRID_W] / sums[2 * i]
            o1 = o2[GRID_W:] / sums[2 * i + 1]
            o_ref[q0:q0 + GRID_W, ls] = jnp.where(first_half, o0, o1).astype(BF16)

    groups = [problems(range(g, g + NA_ROWS_PER_ITER)) for g in range(0, ROWS_PER_TILE, NA_ROWS_PER_ITER)]
    scores = qk(groups[0])
    for gi, work in enumerate(groups):
        nxt_scores = qk(groups[gi + 1]) if gi + 1 < len(groups) else None
        probs, sums = softmax(work, scores)
        pv(work, probs, sums)
        scores = nxt_scores


def _natten(q, k, v, bias, b, seq):
    t = q.shape[0]
    rows = seq // GRID_W
    nblk = seq // NA_TILE
    blk = pl.BlockSpec((NA_TILE, D_NA), lambda bi, j: (bi * nblk + j, 0))
    whole = pl.BlockSpec((seq, D_NA), lambda bi, j: (bi, 0))
    return pl.pallas_call(
        functools.partial(_natten_body, rows=rows),
        grid=(b, nblk),
        in_specs=[blk, whole, whole, _const_spec(bias.shape)],
        out_specs=blk,
        out_shape=jax.ShapeDtypeStruct((t, D_NA), BF16),
        compiler_params=_params(("parallel", "arbitrary")),
        name="natten",
    )(q, k, v, bias)


def _split2_dot(tri, x):
    x0 = x.astype(BF16)
    x1 = (x - x0.astype(F32)).astype(BF16)
    return _dot(tri, x0) + _dot(tri, x1)


def _prep_body(sh_ref, w0_ref, w2_ref, a0_ref, a2_ref, g2_ref,
               kkw_ref, ka_ref, ka1_ref, rk_ref, seg_ref, tri_ref,
               rt_ref, kt_ref, bt_ref, at_ref, v_ref, pend_ref, g_ref, bonus_ref):
    sh = sh_ref[...]
    r = sh[:, 0:D_RWKV]
    k = sh[:, D_RWKV:2 * D_RWKV]
    v = sh[:, 2 * D_RWKV:3 * D_RWKV]
    off = 3 * D_RWKV
    wl = sh[:, off:off + 2 * DECAY_LORA]
    al = sh[:, off + 2 * DECAY_LORA:off + 2 * DECAY_LORA + 2 * ICLR_LORA]
    gl = sh[:, off + 2 * DECAY_LORA + 2 * ICLR_LORA:]

    zw = w0_ref[...] + _dot(jnp.tanh(wl).astype(BF16), w2_ref[...])
    za = a0_ref[...] + _dot(al.astype(BF16), a2_ref[...])
    g_ref[...] = _dot(_sigmoid(gl).astype(BF16), g2_ref[...]).astype(BF16)
    kkv = k * kkw_ref[...]
    kkn = kkv * lax.rsqrt(_dot((kkv * kkv).astype(BF16), seg_ref[...]) + KK_EPS)
    v_ref[...] = v.astype(BF16)
    r_rk = r * rk_ref[...]
    neg_kkn = -kkn
    half_c = -0.5 * math.exp(-0.5) * math.log2(math.e)

    bonus_parts = []
    for c in range(CHUNKS_PER_TILE):
        cs = slice(c * CHUNK, (c + 1) * CHUNK)
        bonus_c = jnp.zeros((CHUNK, D_RWKV), F32)
        for d in range(N_DIR):
            ds_ = slice(d * D_RWKV, (d + 1) * D_RWKV)
            icl = _sigmoid(za[cs, ds_])
            kd = k[cs] * (ka1_ref[...] + icl * ka_ref[...])
            bd = kkn[cs] * icl
            bonus_c = bonus_c + r_rk[cs] * kd
            x = half_c * jnp.tanh(0.5 * zw[cs, ds_]) + half_c
            lp = _split2_dot(tri_ref[d], x)
            lend = lp[CHUNK - 1:CHUNK] if d == 0 else lp[0:1]
            pw = jnp.exp2(lp)
            pinv = 1.0 / pw
            pex = jnp.exp2(lp - x)
            rt_ref[d, cs, :] = (r[cs] * pw).astype(BF16)
            kt_ref[d, cs, :] = (kd * pinv).astype(BF16)
            bt_ref[d, cs, :] = (bd * pinv).astype(BF16)
            at_ref[d, cs, :] = (neg_kkn[cs] * pex).astype(BF16)
            pend_ref[d, c] = jnp.exp2(lend)
        bonus_parts.append(bonus_c)
    bonus_acc = jnp.concatenate(bonus_parts, axis=0)
    bonus_ref[...] = (_dot(bonus_acc.astype(BF16), seg_ref[...]) * v).astype(BF16)


def _prep(sh, w0, w2bd, a0, a2bd, g2, kkw, ka, ka1, rk, seg, tri):
    t = sh.shape[0]
    tb = TOK_TILE
    row = lambda w: pl.BlockSpec((tb, w), lambda i: (i, 0))
    drow = pl.BlockSpec((N_DIR, tb, D_RWKV), lambda i: (0, i, 0))
    consts = [w0, w2bd, a0, a2bd, g2, kkw, ka, ka1, rk, seg, tri]
    bf = jax.ShapeDtypeStruct((N_DIR, t, D_RWKV), BF16)
    return pl.pallas_call(
        _prep_body,
        grid=(t // tb,),
        in_specs=[row(D_SHIFT)] + [_const_spec(c.shape) for c in consts],
        out_specs=[drow] * 4 + [row(D_RWKV),
                                pl.BlockSpec((N_DIR, CHUNKS_PER_TILE, 1, D_RWKV), lambda i: (0, i, 0, 0)),
                                row(D_RWKV), row(D_RWKV)],
        out_shape=[bf] * 4 + [jax.ShapeDtypeStruct((t, D_RWKV), BF16),
                              jax.ShapeDtypeStruct((N_DIR, t // CHUNK, 1, D_RWKV), F32),
                              jax.ShapeDtypeStruct((t, D_RWKV), BF16),
                              jax.ShapeDtypeStruct((t, D_RWKV), BF16)],
        compiler_params=_params(("parallel",)),
        name="rwkv_prep",
    )(sh, *consts)


def _scan_factors(d, rt_ref, kt_ref, bt_ref, at_ref, v_ref, pend_ref, y0_scr, qm_scr, d_scr):
    sign = 1 - 2 * d
    rowi2 = lax.broadcasted_iota(jnp.int32, (CHUNK, 2 * PAIR), 0)
    lane2 = lax.broadcasted_iota(jnp.int32, (CHUNK, 2 * PAIR), 1)
    diff = (rowi2 - (lane2 & (HEAD_DIM - 1))) * sign
    strict2 = diff > 0
    incl2 = diff >= 0
    low2 = (lane2 & HEAD_DIM) == 0
    first_half = lax.broadcasted_iota(jnp.int32, (CHUNK, PAIR), 1) < HEAD_DIM
    rowp = lax.broadcasted_iota(jnp.int32, (PAIR, PAIR), 0)
    lanep = lax.broadcasted_iota(jnp.int32, (PAIR, PAIR), 1)
    same_half = (rowp < HEAD_DIM) == (lanep < HEAD_DIM)
    eye_pair = rowp == lanep
    rowh = lax.broadcasted_iota(jnp.int32, (HALF, PAIR), 0)
    blk = lax.broadcasted_iota(jnp.int32, (HALF, PAIR), 1) // HALF
    first_blk = (blk & 1) == d
    eye_pk = ((lax.broadcasted_iota(jnp.int32, (HALF, PAIR), 1) & (HALF - 1)) == rowh).astype(F32)
    hi_rows = slice((1 - d) * HALF, (2 - d) * HALF)
    zero = jnp.zeros((CHUNK, PAIR), BF16)
    zero2 = jnp.zeros((CHUNK, 2 * PAIR), BF16)
    zero_h = jnp.zeros((HALF, PAIR), BF16)
    cat0 = lambda xs: jnp.concatenate(xs, axis=0)
    cat1 = lambda xs: jnp.concatenate(xs, axis=1)
    sel = lambda a, hh: jnp.where(first_half, a, zero) if hh == 0 else jnp.where(first_half, zero, a)
    lo2 = lambda a: jnp.where(low2, a, zero2)
    hi2 = lambda a: jnp.where(low2, zero2, a)
    only = lambda a, i: jnp.where(blk == i, a, zero_h)
    block_diag4 = lambda a: cat0([only(a, i) for i in range(4)])

    prs = []
    for c in range(CHUNKS_PER_TILE):
        ts = slice(c * CHUNK, (c + 1) * CHUNK)
        for p in range(N_PAIR):
            ls = slice(p * PAIR, (p + 1) * PAIR)
            ld = lambda ref: ref[0, ts, ls]
            prs.append(dict(c=c, ts=ts, p=p, ls=ls, rt=ld(rt_ref), kt=ld(kt_ref), bt=ld(bt_ref),
                            at=ld(at_ref), v=v_ref[ts, ls], pend=pend_ref[0, c, :, ls]))
    for pr in prs:
        rhs = cat0([sel(pr["bt"], 0), sel(pr["kt"], 0), sel(pr["kt"], 1), sel(pr["bt"], 1)])
        aa = lax.dot_general(cat0([pr["at"], pr["rt"]]), rhs, _NT, preferred_element_type=F32)
        top = jnp.where(strict2, aa[0:CHUNK], 0.0)
        pr["top"] = top.astype(BF16)
        pr["bot"] = jnp.where(incl2, aa[CHUNK:], 0.0).astype(BF16)
        t0_, t1_ = top[:, :PAIR], top[:, PAIR:]
        pr["ppk"] = jnp.where(blk == 0, t0_[0:HALF], jnp.where(blk == 1, t0_[HALF:],
                              jnp.where(blk == 2, t1_[0:HALF], t1_[HALF:])))
        pr["ahl"] = jnp.where(blk == d, t0_[hi_rows], jnp.where(blk == 2 + d, t1_[hi_rows], 0.0)).astype(BF16)
        pr["spk"] = eye_pk
    yield
    for pr in prs:
        v, top = pr["v"], pr["top"]
        pr["akv"] = (_dot(top[:, :PAIR], cat0([zero, v])).astype(BF16),
                     _dot(top[:, PAIR:], cat0([v, zero])).astype(BF16))
    yield
    for _ in range(int(math.log2(HALF))):
        res = []
        for pr in prs:
            pb = pr["ppk"].astype(BF16)
            res.append(_dot(pb, cat1([block_diag4(pb), block_diag4(pr["spk"].astype(BF16))])))
        for pr, r in zip(prs, res):
            pr["ppk"] = r[:, :PAIR]
            pr["spk"] = pr["spk"] + r[:, PAIR:]
        yield
    res = []
    for pr in prs:
        sb = pr["spk"].astype(BF16)
        pr["sb"] = sb
        res.append(_dot(pr["ahl"], cat0([jnp.where(first_blk & (blk == i), sb, zero_h) for i in range(4)])))
    yield
    xs = [r.astype(BF16) for r in res]
    res = []
    for pr, xb in zip(prs, xs):
        rows = [zero_h] * 4
        rows[1 - d] = only(xb, d)
        rows[3 - d] = only(xb, 2 + d)
        res.append(_dot(jnp.where(first_blk, zero_h, pr["sb"]), cat0(rows)))
    yield
    for pr, thl in zip(prs, res):
        sb = pr["sb"]
        t_lo = jnp.where(first_blk, sb, zero_h)
        t_hi = jnp.where(first_blk, thl.astype(BF16), sb)
        tinv = cat0([t_lo, t_hi]) if d == 0 else cat0([t_hi, t_lo])
        w0 = cat1([sel(pr["at"], 0), sel(pr["akv"][0], 0)])
        w1 = cat1([sel(pr["at"], 1), sel(pr["akv"][1], 1)])
        pr["au"] = _dot(tinv, cat0([w0, w1])).astype(BF16)
        pr["zv"] = cat1([zero, pr["v"]])
    yield
    for pr in prs:
        au, zv = pr["au"], pr["zv"]
        pr["o"] = _dot(pr["bot"], cat0([lo2(au), lo2(zv), hi2(zv), hi2(au)]))
        bek = cat0([pr["bt"].astype(F32) * pr["pend"], pr["kt"].astype(F32) * pr["pend"]]).astype(BF16)
        pr["md"] = _dot(bek.T, cat0([au, zv]))
    yield
    for pr in prs:
        o, md, ts, ls = pr["o"], pr["md"], pr["ts"], pr["ls"]
        y0_scr[ts, ls] = o[:, PAIR:]
        mbd = jnp.where(eye_pair, pr["pend"], 0.0) + jnp.where(same_half, md[:, :PAIR], 0.0)
        qm_scr[pr["c"], pr["p"]] = cat0([(pr["rt"].astype(F32) + o[:, :PAIR]).astype(BF16), mbd.astype(BF16)])
        d_scr[pr["c"], pr["p"]] = jnp.where(same_half, md[:, PAIR:], 0.0)


def _scan_body(*refs):
    n_in = 6
    ins = (refs[0:n_in], refs[n_in:2 * n_in])
    ys = refs[2 * n_in:2 * n_in + 2]
    h_scr, y0_scr, qm_scr, d_scr = refs[2 * n_in + 2:]

    @pl.when(pl.program_id(1) == 0)
    def _():
        h_scr[...] = jnp.zeros_like(h_scr)

    stages = [_scan_factors(d, *ins[d], y0_scr.at[d], qm_scr.at[d], d_scr.at[d]) for d in range(N_DIR)]
    for _ in itertools.zip_longest(*stages):
        pass

    states = [h_scr[d, p] for d in range(N_DIR) for p in range(N_PAIR)]
    for ci in range(CHUNKS_PER_TILE):
        work = []
        for d in range(N_DIR):
            c = ci if d == 0 else CHUNKS_PER_TILE - 1 - ci
            for p in range(N_PAIR):
                work.append((d, c, slice(c * CHUNK, (c + 1) * CHUNK), p, slice(p * PAIR, (p + 1) * PAIR)))
        res = [_dot(qm_scr[d, c, p], h.astype(BF16)) for (d, c, ts, p, ls), h in zip(work, states)]
        states = [r[CHUNK:] + d_scr[d, c, p] for (d, c, ts, p, ls), r in zip(work, res)]
        for (d, c, ts, p, ls), r in zip(work, res):
            ys[d][ts, ls] = (y0_scr[d, ts, ls] + r[0:CHUNK]).astype(BF16)
    for i, h in enumerate(states):
        h_scr[i // N_PAIR, i % N_PAIR] = h


def _scan(rt, kt, bt, at, v, pend, b, seq):
    t = v.shape[0]
    tc = TOK_TILE
    nblk = seq // tc
    tok = (lambda bi, j: bi * nblk + j, lambda bi, j: bi * nblk + nblk - 1 - j)
    in_specs, args = [], []
    for d in range(N_DIR):
        dblk = pl.BlockSpec((1, tc, D_RWKV), lambda bi, j, d=d: (d, tok[d](bi, j), 0))
        in_specs += [dblk] * 4 + [
            pl.BlockSpec((tc, D_RWKV), lambda bi, j, d=d: (tok[d](bi, j), 0)),
            pl.BlockSpec((1, CHUNKS_PER_TILE, 1, D_RWKV), lambda bi, j, d=d: (d, tok[d](bi, j), 0, 0))]
        args += [rt, kt, bt, at, v, pend]
    return pl.pallas_call(
        _scan_body,
        grid=(b, nblk),
        in_specs=in_specs,
        out_specs=[pl.BlockSpec((tc, D_RWKV), lambda bi, j, d=d: (tok[d](bi, j), 0)) for d in range(N_DIR)],
        out_shape=[jax.ShapeDtypeStruct((t, D_RWKV), BF16)] * N_DIR,
        scratch_shapes=[pltpu.VMEM((N_DIR, N_PAIR, PAIR, PAIR), F32),
                        pltpu.VMEM((N_DIR, tc, D_RWKV), F32),
                        pltpu.VMEM((N_DIR, CHUNKS_PER_TILE, N_PAIR, CHUNK + PAIR, PAIR), BF16),
                        pltpu.VMEM((N_DIR, CHUNKS_PER_TILE, N_PAIR, PAIR, PAIR), F32)],
        compiler_params=_params(("parallel", "arbitrary")),
        name="rwkv_scan",
    )(*args)


def _merge_body(y0_ref, y1_ref, bonus_ref, g_ref, ona_ref, gt_ref, x_ref, seg_ref, lnw_ref, lnb_ref,
                wpa_ref, wpb_ref, wout_ref, npost_ref, o_ref):
    inv_n = 1.0 / HEAD_DIM
    y = y0_ref[...].astype(F32) + y1_ref[...]
    mean = _dot(y.astype(BF16), seg_ref[...]) * inv_n
    yc = y - mean
    var = _dot((yc * yc).astype(BF16), seg_ref[...]) * inv_n
    yn = yc * lax.rsqrt(var + GN_EPS) * lnw_ref[...] + lnb_ref[...]
    orw = ((yn + bonus_ref[...]) * g_ref[...]).astype(BF16)
    gates = gt_ref[...]
    merged = (gates[:, :D_MODEL] * _dot(ona_ref[...], wpa_ref[...])
              + gates[:, D_MODEL:] * _dot(orw, wpb_ref[...]))
    z = _dot(merged.astype(BF16), wout_ref[...])
    ms = jnp.mean(z * z, axis=-1, keepdims=True)
    o_ref[...] = x_ref[...] + z * lax.rsqrt(ms + RMS_EPS) * npost_ref[...]


def _merge(y0, y1, bonus, g, ona, gates, x, seg, lnw, lnb, wpa, wpb, wout, npost):
    t = x.shape[0]
    tm = TOK_TILE
    row = lambda w: pl.BlockSpec((tm, w), lambda i: (i, 0))
    consts = [seg, lnw, lnb, wpa, wpb, wout, npost]
    return pl.pallas_call(
        _merge_body,
        grid=(t // tm,),
        in_specs=[row(D_RWKV), row(D_RWKV), row(D_RWKV), row(D_RWKV), row(D_NA), row(2 * D_MODEL), row(D_MODEL)]
        + [_const_spec(c.shape) for c in consts],
        out_specs=row(D_MODEL),
        out_shape=jax.ShapeDtypeStruct((t, D_MODEL), F32),
        compiler_params=_params(("parallel",)),
        name="merge",
    )(y0, y1, bonus, g, ona, gates, x, *consts)


def _ffn_body(x_ref, npre_ref, wup_ref, wdn_ref, npost_ref, o_ref):
    x = x_ref[...]
    ms = jnp.mean(x * x, axis=-1, keepdims=True)
    h = (x * lax.rsqrt(ms + RMS_EPS) * npre_ref[...]).astype(BF16)
    f = jnp.zeros_like(x)
    for s_ in range(FFN_SPLIT):
        cols = slice(s_ * (D_FF // FFN_SPLIT), (s_ + 1) * (D_FF // FFN_SPLIT))
        u = jnp.maximum(_dot(h, wup_ref[:, cols]), 0.0)
        f = f + _dot((u * u).astype(BF16), wdn_ref[cols, :])
    ms = jnp.mean(f * f, axis=-1, keepdims=True)
    o_ref[...] = x + f * lax.rsqrt(ms + RMS_EPS) * npost_ref[...]


def _ffn(x, npre, wup, wdn, npost):
    t = x.shape[0]
    tm = FFN_TILE
    row = pl.BlockSpec((tm, D_MODEL), lambda i: (i, 0))
    consts = [npre, wup, wdn, npost]
    return pl.pallas_call(
        _ffn_body,
        grid=(t // tm,),
        in_specs=[row] + [_const_spec(c.shape) for c in consts],
        out_specs=row,
        out_shape=jax.ShapeDtypeStruct((t, D_MODEL), F32),
        compiler_params=_params(("parallel",)),
        name="ffn",
    )(x, *consts)


def _block_diag2(w):
    z = jnp.zeros_like(w[0])
    return jnp.concatenate([jnp.concatenate([w[0], z], axis=1),
                            jnp.concatenate([z, w[1]], axis=1)], axis=0)


def _layer_consts(w_in, b_gate, rpb, mu_prev, mu_next, w0, w2, a0, a2, g2, k_k, k_a, r_k,
                  ln_w, ln_b, w_proj_a, w_proj_b, w_out, w_up, w_down,
                  norm_pre_mix, norm_post_mix, norm_pre_ffn, norm_post_ffn):
    rowv = lambda a: a.reshape(1, -1).astype(F32)
    head = jnp.arange(D_RWKV) // HEAD_DIM
    idx = jnp.arange(CHUNK)
    tril = (idx[:, None] >= idx[None, :])
    return dict(
        wmain=jnp.concatenate([w_in[:, :3 * D_NA], w_in[:, 3 * D_NA + D_SHIFT:]], axis=1).astype(BF16),
        wsh=w_in[:, 3 * D_NA:3 * D_NA + D_SHIFT].astype(BF16),
        bg=rowv(b_gate), bias=_na_bias_table(rpb),
        mu0=rowv(1.0 - mu_prev - mu_next), mup=rowv(mu_prev), mun=rowv(mu_next),
        w0=rowv(w0), w2bd=_block_diag2(w2).astype(BF16),
        a0=rowv(a0), a2bd=_block_diag2(a2).astype(BF16),
        g2=g2.astype(BF16), kkw=rowv(k_k), ka=rowv(k_a), ka1=rowv(1.0 - k_a), rk=rowv(r_k),
        seg=(head[:, None] == head[None, :]).astype(BF16),
        tri=jnp.stack([tril, tril.T]).astype(BF16),
        lnw=rowv(ln_w), lnb=rowv(ln_b),
        wpa=w_proj_a.astype(BF16), wpb=w_proj_b.astype(BF16), wout=w_out.astype(BF16),
        wup=w_up.astype(BF16), wdn=w_down.astype(BF16),
        npre_mix=rowv(norm_pre_mix), npost_mix=rowv(norm_post_mix),
        npre_ffn=rowv(norm_pre_ffn), npost_ffn=rowv(norm_post_ffn),
    )


def _encoder_layer(x, b, seq, c):
    q, k, v, sh, gates = _inproj(x, seq, c["npre_mix"], c["wmain"], c["wsh"], c["bg"],
                                 c["mu0"], c["mup"], c["mun"])
    ona = _natten(q, k, v, c["bias"], b, seq)
    rt, kt, bt, at, vr, pend, g, bonus = _prep(
        sh, c["w0"], c["w2bd"], c["a0"], c["a2bd"], c["g2"],
        c["kkw"], c["ka"], c["ka1"], c["rk"], c["seg"], c["tri"])
    y0, y1 = _scan(rt, kt, bt, at, vr, pend, b, seq)
    x1 = _merge(y0, y1, bonus, g, ona, gates, x, c["seg"], c["lnw"], c["lnb"],
                c["wpa"], c["wpb"], c["wout"], c["npost_mix"])
    return _ffn(x1, c["npre_ffn"], c["wup"], c["wdn"], c["npost_ffn"])


def _trunk(x, layer_consts):
    b, seq, dm = x.shape
    assert dm == D_MODEL and seq % TOK_TILE == 0 and seq % NA_TILE == 0 and seq // GRID_W >= WIN_H
    h = x.reshape(b * seq, dm)
    for c in layer_consts:
        h = _encoder_layer(h, b, seq, c)
    return h.reshape(b, seq, dm)


def kernel(x_prompt, x_sample, w_in, b_gate, rpb, mu_prev, mu_next, w0, w2, a0, a2, g2, k_k, k_a, r_k, ln_w, ln_b, w_proj_a, w_proj_b, w_out, w_up, w_down, norm_pre_mix, norm_post_mix, norm_pre_ffn, norm_post_ffn):
    stacked = (w_in, b_gate, rpb, mu_prev, mu_next, w0, w2, a0, a2, g2, k_k, k_a, r_k,
               ln_w, ln_b, w_proj_a, w_proj_b, w_out, w_up, w_down,
               norm_pre_mix, norm_post_mix, norm_pre_ffn, norm_post_ffn)
    depth = w_in.shape[0]
    layer_consts = [_layer_consts(*[p[i] for p in stacked]) for i in range(depth)]
    return (_trunk(x_prompt, layer_consts), _trunk(x_sample, layer_consts))
```
